```python
import math
import jax, jax.numpy as jnp
from jax import lax
import numpy as np

D_MODEL = 1024
BATCH = 2
SEQ = 16384
DEPTH = 4
DEC_BATCH = 32
DEC_SEQ = 2048
PAST_LEN = 128

CONV_WIDTH = D_MODEL // 4
CONV_K = 31
DIFF_HEADS = 4
DIFF_HEAD_DIM = 64
DIFF_V_DIM = 2 * DIFF_HEAD_DIM
GQA_Q_HEADS = 4
GQA_KV_HEADS = 2
GQA_HEAD_DIM = 64
MIX_WIDTH = CONV_WIDTH + DIFF_HEADS * DIFF_V_DIM + GQA_Q_HEADS * GQA_HEAD_DIM
FFN_HIDDEN = 2816
FFN_CONV_K = 3
ROPE_THETA = 500000.0
PARTIAL_ROT_DIM = DIFF_HEAD_DIM // 4
AXIAL_THETA = 10000.0
AXIAL_HALF = GQA_HEAD_DIM // 2
GRID_W = 64
Q_BLOCK = 128
EPS = 1e-6

A_VAL_W = CONV_WIDTH
A_GATE_W = CONV_WIDTH
B_Q_W = DIFF_HEADS * 2 * DIFF_HEAD_DIM
B_K_W = DIFF_HEADS * 2 * DIFF_HEAD_DIM
B_V_W = DIFF_HEADS * DIFF_V_DIM
C_Q_W = GQA_Q_HEADS * GQA_HEAD_DIM
C_K_W = GQA_KV_HEADS * GQA_HEAD_DIM
C_V_W = GQA_KV_HEADS * GQA_HEAD_DIM
IN_WIDTH = A_VAL_W + A_GATE_W + B_Q_W + B_K_W + B_V_W + C_Q_W + C_K_W + C_V_W
SPLITS = (
    A_VAL_W,
    A_VAL_W + A_GATE_W,
    A_VAL_W + A_GATE_W + B_Q_W,
    A_VAL_W + A_GATE_W + B_Q_W + B_K_W,
    A_VAL_W + A_GATE_W + B_Q_W + B_K_W + B_V_W,
    A_VAL_W + A_GATE_W + B_Q_W + B_K_W + B_V_W + C_Q_W,
    A_VAL_W + A_GATE_W + B_Q_W + B_K_W + B_V_W + C_Q_W + C_K_W,
)

kernel_name = "hymba_style_conv_diffattn_axialgqa_encoder"


def rms_norm(x, g):
    xf = x.astype(jnp.float32)
    y = xf * lax.rsqrt(jnp.mean(xf * xf, axis=-1, keepdims=True) + EPS)
    return (y * g.astype(jnp.float32)).astype(x.dtype)


def layer_norm(x, g, b):
    xf = x.astype(jnp.float32)
    mu = jnp.mean(xf, axis=-1, keepdims=True)
    xc = xf - mu
    var = jnp.mean(xc * xc, axis=-1, keepdims=True)
    y = xc * lax.rsqrt(var + EPS) * g.astype(jnp.float32) + b.astype(jnp.float32)
    return y.astype(x.dtype)


def depthwise_conv(x, w, b):
    k = w.shape[0]
    pad = k // 2
    y = lax.conv_general_dilated(
        x, w[:, None, :].astype(x.dtype), window_strides=(1,), padding=[(pad, pad)],
        dimension_numbers=("NWC", "WIO", "NWC"), feature_group_count=x.shape[-1])
    return y + b.astype(x.dtype)


def rope_angles(pos, dim, theta):
    inv = theta ** (-jnp.arange(0, dim, 2, dtype=jnp.float32) / dim)
    ang = pos.astype(jnp.float32)[:, None] * inv[None, :]
    return jnp.cos(ang), jnp.sin(ang)


def rotate(x, cos, sin):
    half = x.shape[-1] // 2
    shape = (1, cos.shape[0]) + (1,) * (x.ndim - 3) + (cos.shape[1],)
    c = cos.reshape(shape).astype(x.dtype)
    s = sin.reshape(shape).astype(x.dtype)
    x1, x2 = x[..., :half], x[..., half:]
    return jnp.concatenate([x1 * c - x2 * s, x2 * c + x1 * s], axis=-1)


def to_blocks(t):
    b, s = t.shape[:2]
    t = t.reshape((b, s // Q_BLOCK, Q_BLOCK) + t.shape[2:])
    return jnp.moveaxis(t, 1, 0)


def from_blocks(t):
    t = jnp.moveaxis(t, 0, 1)
    return t.reshape((t.shape[0], t.shape[1] * t.shape[2]) + t.shape[3:])


def diff_attention(q1, q2, k1, k2, v, lam):
    scale = DIFF_HEAD_DIM ** -0.5
    q1 = q1 * scale
    q2 = q2 * scale

    def one_block(qb):
        qb1, qb2 = qb
        s1 = jnp.einsum("bqhd,bkhd->bhqk", qb1, k1).astype(jnp.float32)
        s2 = jnp.einsum("bqhd,bkhd->bhqk", qb2, k2).astype(jnp.float32)
        w = jax.nn.softmax(s1, axis=-1) - lam * jax.nn.softmax(s2, axis=-1)
        return jnp.einsum("bhqk,bkhe->bqhe", w.astype(v.dtype), v)

    return from_blocks(lax.map(one_block, (to_blocks(q1), to_blocks(q2))))


def gqa_attention(q, k, v):
    b, s = q.shape[:2]
    rep = GQA_Q_HEADS // GQA_KV_HEADS
    qg = q.reshape(b, s, GQA_KV_HEADS, rep, GQA_HEAD_DIM) * (GQA_HEAD_DIM ** -0.5)

    def one_block(qb):
        sc = jnp.einsum("bqgrd,bkgd->bgrqk", qb, k).astype(jnp.float32)
        p = jax.nn.softmax(sc, axis=-1)
        return jnp.einsum("bgrqk,bkgd->bqgrd", p.astype(v.dtype), v)

    out = from_blocks(lax.map(one_block, to_blocks(qg)))
    return out.reshape(b, s, GQA_Q_HEADS * GQA_HEAD_DIM)


def encoder_layer(x, layer_idx, norm1_g, w_in, conv_a_w, conv_a_b, ln_a_g, ln_a_b,
                  qn_b_g, kn_b_g, lam_q1, lam_k1, lam_q2, lam_k2, subln_b_g,
                  qn_c_g, kn_c_g, w_out, norm2_g, w_up, conv_f_w, conv_f_b, w_down):
    b, s, _ = x.shape
    rows = s // GRID_W
    pos = jnp.arange(s, dtype=jnp.int32)
    row_ids = jnp.repeat(jnp.arange(rows, dtype=jnp.int32), GRID_W)
    col_ids = jnp.tile(jnp.arange(GRID_W, dtype=jnp.int32), rows)

    h = rms_norm(x, norm1_g)
    proj = h @ w_in
    a_val, a_gate, bq, bk, bv, cq, ck, cv = jnp.split(proj, SPLITS, axis=-1)

    a = a_val * jax.nn.sigmoid(a_gate)
    a = depthwise_conv(a, conv_a_w, conv_a_b)
    a = jax.nn.silu(layer_norm(a, ln_a_g, ln_a_b))

    bq = rms_norm(bq.reshape(b, s, DIFF_HEADS, 2, DIFF_HEAD_DIM), qn_b_g)
    bk = rms_norm(bk.reshape(b, s, DIFF_HEADS, 2, DIFF_HEAD_DIM), kn_b_g)
    cos_p, sin_p = rope_angles(pos, PARTIAL_ROT_DIM, ROPE_THETA)
    bq = jnp.concatenate([rotate(bq[..., :PARTIAL_ROT_DIM], cos_p, sin_p), bq[..., PARTIAL_ROT_DIM:]], axis=-1)
    bk = jnp.concatenate([rotate(bk[..., :PARTIAL_ROT_DIM], cos_p, sin_p), bk[..., PARTIAL_ROT_DIM:]], axis=-1)
    lam_init = 0.8 - 0.6 * math.exp(-0.3 * layer_idx)
    lam = (jnp.exp(jnp.sum(lam_q1.astype(jnp.float32) * lam_k1.astype(jnp.float32)))
           - jnp.exp(jnp.sum(lam_q2.astype(jnp.float32) * lam_k2.astype(jnp.float32)))
           + lam_init)
    ob = diff_attention(bq[..., 0, :], bq[..., 1, :], bk[..., 0, :], bk[..., 1, :],
                        bv.reshape(b, s, DIFF_HEADS, DIFF_V_DIM), lam)
    ob = (rms_norm(ob, subln_b_g) * (1.0 - lam_init)).reshape(b, s, DIFF_HEADS * DIFF_V_DIM)

    cq = rms_norm(cq.reshape(b, s, GQA_Q_HEADS, GQA_HEAD_DIM), qn_c_g)
    ck = rms_norm(ck.reshape(b, s, GQA_KV_HEADS, GQA_HEAD_DIM), kn_c_g)
    cos_r, sin_r = rope_angles(row_ids, AXIAL_HALF, AXIAL_THETA)
    cos_c, sin_c = rope_angles(col_ids, AXIAL_HALF, AXIAL_THETA)

    def axial(t):
        return jnp.concatenate([rotate(t[..., :AXIAL_HALF], cos_r, sin_r),
                                rotate(t[..., AXIAL_HALF:], cos_c, sin_c)], axis=-1)

    oc = gqa_attention(axial(cq), axial(ck), cv.reshape(b, s, GQA_KV_HEADS, GQA_HEAD_DIM))

    x = x + jnp.concatenate([a, ob, oc], axis=-1) @ w_out

    u = depthwise_conv(rms_norm(x, norm2_g) @ w_up, conv_f_w, conv_f_b)
    u_val, u_gate = jnp.split(u, 2, axis=-1)
    return x + (u_val * jax.nn.silu(u_gate)) @ w_down


def setup_inputs(seed: int = 0) -> dict:
    key = jax.random.key(seed)
    ks = jax.random.split(key, 24)
    f32 = jnp.float32

    def nrm(k, shape, scale):
        return jax.random.normal(k, shape, f32) * scale

    def gain(k, shape):
        return 1.0 + 0.02 * jax.random.normal(k, shape, f32)

    return {
        "x_prompt": nrm(ks[0], (BATCH, SEQ, D_MODEL), 1.0),
        "x_sample": nrm(ks[1], (DEC_BATCH, DEC_SEQ, D_MODEL), 1.0),
        "norm1_g": gain(ks[2], (DEPTH, D_MODEL)),
        "w_in": nrm(ks[3], (DEPTH, D_MODEL, IN_WIDTH), D_MODEL ** -0.5),
        "conv_a_w": nrm(ks[4], (DEPTH, CONV_K, CONV_WIDTH), CONV_K ** -0.5),
        "conv_a_b": nrm(ks[5], (DEPTH, CONV_WIDTH), 0.01),
        "ln_a_g": gain(ks[6], (DEPTH, CONV_WIDTH)),
        "ln_a_b": nrm(ks[7], (DEPTH, CONV_WIDTH), 0.01),
        "qn_b_g": gain(ks[8], (DEPTH, DIFF_HEAD_DIM)),
        "kn_b_g": gain(ks[9], (DEPTH, DIFF_HEAD_DIM)),
        "lam_q1": nrm(ks[10], (DEPTH, DIFF_HEAD_DIM), 0.1),
        "lam_k1": nrm(ks[11], (DEPTH, DIFF_HEAD_DIM), 0.1),
        "lam_q2": nrm(ks[12], (DEPTH, DIFF_HEAD_DIM), 0.1),
        "lam_k2": nrm(ks[13], (DEPTH, DIFF_HEAD_DIM), 0.1),
        "subln_b_g": gain(ks[14], (DEPTH, DIFF_V_DIM)),
        "qn_c_g": gain(ks[15], (DEPTH, GQA_HEAD_DIM)),
        "kn_c_g": gain(ks[16], (DEPTH, GQA_HEAD_DIM)),
        "w_out": nrm(ks[17], (DEPTH, MIX_WIDTH, D_MODEL), MIX_WIDTH ** -0.5),
        "norm2_g": gain(ks[18], (DEPTH, D_MODEL)),
        "w_up": nrm(ks[19], (DEPTH, D_MODEL, 2 * FFN_HIDDEN), D_MODEL ** -0.5),
        "conv_f_w": nrm(ks[20], (DEPTH, FFN_CONV_K, 2 * FFN_HIDDEN), FFN_CONV_K ** -0.5),
        "conv_f_b": nrm(ks[21], (DEPTH, 2 * FFN_HIDDEN), 0.01),
        "w_down": nrm(ks[22], (DEPTH, FFN_HIDDEN, D_MODEL), FFN_HIDDEN ** -0.5),
    }


def reference(x_prompt, x_sample, norm1_g, w_in, conv_a_w, conv_a_b, ln_a_g, ln_a_b,
              qn_b_g, kn_b_g, lam_q1, lam_k1, lam_q2, lam_k2, subln_b_g,
              qn_c_g, kn_c_g, w_out, norm2_g, w_up, conv_f_w, conv_f_b, w_down):
    y_prompt = x_prompt
    y_sample = x_sample
    for l in range(DEPTH):
        layer_params = (norm1_g[l], w_in[l], conv_a_w[l], conv_a_b[l], ln_a_g[l], ln_a_b[l],
                        qn_b_g[l], kn_b_g[l], lam_q1[l], lam_k1[l], lam_q2[l], lam_k2[l],
                        subln_b_g[l], qn_c_g[l], kn_c_g[l], w_out[l], norm2_g[l], w_up[l],
                        conv_f_w[l], conv_f_b[l], w_down[l])
        y_prompt = encoder_layer(y_prompt, l, *layer_params)
        y_sample = encoder_layer(y_sample, l, *layer_params)
    return (y_prompt, y_sample)
```

```python
import functools
import math

import jax
import jax.numpy as jnp
from jax import lax
from jax.experimental import pallas as pl
from jax.experimental.pallas import tpu as pltpu

F32 = jnp.float32
BF16 = jnp.bfloat16

D_MODEL = 1024
DEPTH = 4
CONV_WIDTH = 256
CONV_K = 31
DIFF_HEADS = 4
HEAD_DIM = 64
DIFF_V_DIM = 128
GQA_KV_HEADS = 2
FFN_HIDDEN = 2816
ROPE_THETA = 500000.0
PARTIAL_ROT_DIM = 16
AXIAL_THETA = 10000.0
AXIAL_HALF = 32
GRID_W = 64
EPS = 1e-6

LANES = 128
SUBLANES = 8
VMEM_LIMIT = 56 * 1024 * 1024

TM_PROJ = 512
TM_FFN = 512
FFN_CHUNK = 256
N_FFN_CHUNKS = FFN_HIDDEN // FFN_CHUNK
HALO = 8
A_HALO = 16

NT_DIMS = (((1,), (1,)), ((), ()))


def _rms_rows(x, gain):
    ms = jnp.mean(x * x, axis=-1, keepdims=True)
    return x * lax.rsqrt(ms + EPS) * gain


def _sigmoid(x):
    return 1.0 / (1.0 + jnp.exp(-x))


def _chunk_norm_rope(blk, ones, gain, cos, sin_up, sin_dn, shift):
    sq = blk * blk
    hi = sq.astype(BF16)
    lo = (sq - hi.astype(F32)).astype(BF16)
    ss = (jnp.dot(hi, ones, preferred_element_type=F32)
          + jnp.dot(lo, ones, preferred_element_type=F32))
    y = blk * lax.rsqrt(ss * (1.0 / HEAD_DIM) + EPS) * gain
    pieces = []
    for j in range(blk.shape[1] // LANES):
        yj = y[:, j * LANES:(j + 1) * LANES]
        pieces.append(yj * cos
                      + pltpu.roll(yj, shift, 1) * sin_up
                      + pltpu.roll(yj, LANES - shift, 1) * sin_dn)
    return pieces


def _inproj_kernel(x_ref, g1_ref, w_ref, wvt_ref, gb_ref, gc_ref, ones_ref,
                   cb_ref, sub_ref, sdb_ref, cc_ref, suc_ref, sdc_ref,
                   a_ref, qb_ref, kb_ref, vbt_ref, qc_ref, kc_ref, vct_ref):
    h = _rms_rows(x_ref[0], g1_ref[...]).astype(BF16)

    pa = jnp.dot(h, w_ref[:, 0:512], preferred_element_type=F32)
    a_ref[0] = pa[:, :CONV_WIDTH] * _sigmoid(pa[:, CONV_WIDTH:])

    ones = ones_ref[...]
    cb, sub, sdb = cb_ref[...], sub_ref[...], sdb_ref[...]
    cc, suc, sdc = cc_ref[...], suc_ref[...], sdc_ref[...]

    for blk_i in range(4):
        c0 = 512 + 256 * blk_i
        blk = jnp.dot(h, w_ref[:, c0:c0 + 256], preferred_element_type=F32)
        pieces = _chunk_norm_rope(blk, ones, gb_ref[:, 256 * blk_i:256 * (blk_i + 1)],
                                  cb, sub, sdb, PARTIAL_ROT_DIM // 2)
        dst = qb_ref if blk_i < 2 else kb_ref
        for j, pc in enumerate(pieces):
            lane0 = (256 * blk_i) % 512 + LANES * j
            dst[0, :, lane0:lane0 + LANES] = pc.astype(BF16)

    for blk_i in range(2):
        c0 = 1536 + 256 * blk_i
        blk = jnp.dot(h, w_ref[:, c0:c0 + 256], preferred_element_type=F32)
        pieces = _chunk_norm_rope(blk, ones, gc_ref[:, 256 * blk_i:256 * (blk_i + 1)],
                                  cc, suc, sdc, AXIAL_HALF // 2)
        dst = qc_ref if blk_i == 0 else kc_ref
        for j, pc in enumerate(pieces):
            dst[0, :, LANES * j:LANES * (j + 1)] = pc.astype(BF16)

    vt = lax.dot_general(wvt_ref[...], h, NT_DIMS, preferred_element_type=F32)
    vbt_ref[0, 0] = vt[0:512].astype(BF16)
    vct_ref[0, 0] = vt[512:640].astype(BF16)


def _inproj_call(x, g1, w_cols, w_vt, gb, gc, ones, tabs):
    B, S, _ = x.shape
    tm = TM_PROJ
    nt = S // tm
    tok = lambda w: pl.BlockSpec((1, tm, w), lambda b, i: (b, i, 0))
    full = lambda a: pl.BlockSpec(a.shape, lambda b, i: (0,) * a.ndim)
    tab = pl.BlockSpec((tm, LANES), lambda b, i: (i, 0))
    out_shape = (
        jax.ShapeDtypeStruct((B, S, CONV_WIDTH), F32),
        jax.ShapeDtypeStruct((B, S, 512), BF16),
        jax.ShapeDtypeStruct((B, S, 512), BF16),
        jax.ShapeDtypeStruct((B, nt, 512, tm), BF16),
        jax.ShapeDtypeStruct((B, S, 256), BF16),
        jax.ShapeDtypeStruct((B, S, 256), BF16),
        jax.ShapeDtypeStruct((B, nt, 128, tm), BF16),
    )
    out_specs = (
        tok(CONV_WIDTH), tok(512), tok(512),
        pl.BlockSpec((1, 1, 512, tm), lambda b, i: (b, i, 0, 0)),
        tok(256), tok(256),
        pl.BlockSpec((1, 1, 128, tm), lambda b, i: (b, i, 0, 0)),
    )
    return pl.pallas_call(
        _inproj_kernel,
        grid=(B, nt),
        in_specs=[tok(D_MODEL), full(g1), full(w_cols), full(w_vt), full(gb), full(gc), full(ones)]
                 + [tab] * 6,
        out_specs=out_specs,
        out_shape=out_shape,
        compiler_params=pltpu.CompilerParams(
            dimension_semantics=("parallel", "parallel"), vmem_limit_bytes=VMEM_LIMIT),
        name="inproj",
    )(x, g1, w_cols, w_vt, gb, gc, ones, *tabs)


def _flash_core(q_ref, k_ref, vt_ref, qop_ref, m_ref, l_ref, acc_ref, tq, tk, n_chunks):
    q = q_ref[0].astype(F32)
    lane = lax.broadcasted_iota(jnp.int32, q.shape, 1)
    qop_ref[0:tq, :] = jnp.where(lane < HEAD_DIM, q, 0.0).astype(BF16)
    qop_ref[tq:2 * tq, :] = jnp.where(lane >= HEAD_DIM, q, 0.0).astype(BF16)
    m_ref[...] = jnp.full(m_ref.shape, -jnp.inf, F32)
    l_ref[...] = jnp.zeros(l_ref.shape, F32)
    acc_ref[...] = jnp.zeros(acc_ref.shape, F32)

    def body(c, carry):
        k = k_ref[0, pl.ds(pl.multiple_of(c * tk, tk), tk), :]
        s = lax.dot_general(k, qop_ref[...], NT_DIMS, preferred_element_type=F32)
        m_old = m_ref[...]
        m_new = jnp.maximum(m_old, jnp.max(s, axis=0, keepdims=True))
        alpha = jnp.exp(m_old - m_new)
        p = jnp.exp(s - m_new)
        l_ref[...] = alpha * l_ref[...] + jnp.sum(p, axis=0, keepdims=True)
        pv = jnp.dot(vt_ref[0, c], p.astype(BF16), preferred_element_type=F32)
        acc_ref[...] = alpha * acc_ref[...] + pv
        m_ref[...] = m_new
        return carry

    lax.fori_loop(0, n_chunks, body, 0)
    return acc_ref[...] * (1.0 / l_ref[...])


def _diff_attn_kernel(q_ref, k_ref, vt_ref, lam_ref, lamc_ref, g_ref, o_ref,
                      qop_ref, m_ref, l_ref, acc_ref, *, tq, tk, n_chunks):
    o = _flash_core(q_ref, k_ref, vt_ref, qop_ref, m_ref, l_ref, acc_ref, tq, tk, n_chunks)
    lam_init = lamc_ref[0:1, 0:1]
    lam = (jnp.exp(jnp.sum(lam_ref[0:1, :] * lam_ref[1:2, :], axis=-1, keepdims=True))
           - jnp.exp(jnp.sum(lam_ref[2:3, :] * lam_ref[3:4, :], axis=-1, keepdims=True))
           + lam_init)
    od = o[:, 0:tq] - lam * o[:, tq:2 * tq]
    ms = jnp.mean(od * od, axis=0, keepdims=True)
    y = od * lax.rsqrt(ms + EPS) * g_ref[...] * (1.0 - lam_init)
    o_ref[0] = y.T


def _gqa_attn_kernel(q_ref, k_ref, vt_ref, o_ref, qop_ref, m_ref, l_ref, acc_ref, *, tq, tk, n_chunks):
    o = _flash_core(q_ref, k_ref, vt_ref, qop_ref, m_ref, l_ref, acc_ref, tq, tk, n_chunks)
    o_ref[0] = jnp.concatenate([o[:, 0:tq], o[:, tq:2 * tq]], axis=0).T


def _attn_tq(S):
    return 256


def _attn_scratch(tq, dv):
    return [pltpu.VMEM((2 * tq, LANES), BF16), pltpu.VMEM((1, 2 * tq), F32),
            pltpu.VMEM((1, 2 * tq), F32), pltpu.VMEM((dv, 2 * tq), F32)]


def _diff_attn_call(qb, kb, vbt, lam_vecs, lam_consts, subln_col):
    B, S, _ = qb.shape
    tq, tk = _attn_tq(S), TM_PROJ
    nc = S // tk
    kern = functools.partial(_diff_attn_kernel, tq=tq, tk=tk, n_chunks=nc)
    return pl.pallas_call(
        kern,
        grid=(B, DIFF_HEADS, S // tq),
        in_specs=[
            pl.BlockSpec((1, tq, LANES), lambda b, h, i: (b, i, h)),
            pl.BlockSpec((1, S, LANES), lambda b, h, i: (b, 0, h)),
            pl.BlockSpec((1, nc, DIFF_V_DIM, tk), lambda b, h, i: (b, 0, h, 0)),
            pl.BlockSpec(lam_vecs.shape, lambda b, h, i: (0, 0)),
            pl.BlockSpec(lam_consts.shape, lambda b, h, i: (0, 0)),
            pl.BlockSpec(subln_col.shape, lambda b, h, i: (0, 0)),
        ],
        out_specs=pl.BlockSpec((1, tq, LANES), lambda b, h, i: (b, i, h)),
        out_shape=jax.ShapeDtypeStruct((B, S, DIFF_HEADS * DIFF_V_DIM), F32),
        scratch_shapes=_attn_scratch(tq, DIFF_V_DIM),
        compiler_params=pltpu.CompilerParams(
            dimension_semantics=("parallel", "parallel", "parallel"), vmem_limit_bytes=VMEM_LIMIT),
        name="diff_attn",
    )(qb, kb, vbt, lam_vecs, lam_consts, subln_col)


def _gqa_attn_call(qc, kc, vct):
    B, S, _ = qc.shape
    tq, tk = _attn_tq(S), TM_PROJ
    nc = S // tk
    kern = functools.partial(_gqa_attn_kernel, tq=tq, tk=tk, n_chunks=nc)
    return pl.pallas_call(
        kern,
        grid=(B, GQA_KV_HEADS, S // tq),
        in_specs=[
            pl.BlockSpec((1, tq, LANES), lambda b, g, i: (b, i, g)),
            pl.BlockSpec((1, S, LANES), lambda b, g, i: (b, 0, g)),
            pl.BlockSpec((1, nc, HEAD_DIM, tk), lambda b, g, i: (b, 0, g, 0)),
        ],
        out_specs=pl.BlockSpec((1, tq, LANES), lambda b, g, i: (b, i, g)),
        out_shape=jax.ShapeDtypeStruct((B, S, 2 * LANES), F32),
        scratch_shapes=_attn_scratch(tq, HEAD_DIM),
        compiler_params=pltpu.CompilerParams(
            dimension_semantics=("parallel", "parallel", "parallel"), vmem_limit_bytes=VMEM_LIMIT),
        name="gqa_attn",
    )(qc, kc, vct)


def _mix_ffn_kernel(x_ref, xp_ref, xn_ref, a_ref, ap_ref, an_ref,
                    ob_ref, obp_ref, obn_ref, oc_ref, ocp_ref, ocn_ref,
                    caw_ref, cab_ref, lng_ref, lnb_ref, wout_ref, g2_ref,
                    wup_ref, cfw_ref, cfb_ref, wdn_ref,
                    o_ref, as_ref, h2_ref, acc_ref, *, tm, seq_len):
    rows = tm + 2 * HALO
    t0 = pl.program_id(1) * tm

    as_ref[0:8, :] = jnp.zeros((8, CONV_WIDTH), F32)
    as_ref[8:8 + A_HALO, :] = ap_ref[0]
    as_ref[8 + A_HALO:8 + A_HALO + tm, :] = a_ref[0]
    as_ref[8 + A_HALO + tm:8 + 2 * A_HALO + tm, :] = an_ref[0]
    as_ref[8 + 2 * A_HALO + tm:16 + 2 * A_HALO + tm, :] = jnp.zeros((8, CONV_WIDTH), F32)
    arow = lax.broadcasted_iota(jnp.int32, (tm + 2 * A_HALO + 16, 1), 0) + (t0 - A_HALO - 8)
    a_valid = jnp.logical_and(arow >= 0, arow < seq_len)
    as_ref[...] = jnp.where(a_valid, as_ref[...], 0.0)

    conv = jnp.zeros((rows, CONV_WIDTH), F32) + cab_ref[...]
    for k in range(CONV_K):
        conv = conv + as_ref[k + 1:k + 1 + rows, :] * caw_ref[k:k + 1, :]
    mu = jnp.mean(conv, axis=-1, keepdims=True)
    xc = conv - mu
    var = jnp.mean(xc * xc, axis=-1, keepdims=True)
    ln = xc * lax.rsqrt(var + EPS) * lng_ref[...] + lnb_ref[...]
    a_mix = (ln * _sigmoid(ln)).astype(BF16)

    cat = lambda p, c, n: jnp.concatenate([p[0], c[0], n[0]], axis=0)
    x_ext = cat(xp_ref, x_ref, xn_ref)
    ob = cat(obp_ref, ob_ref, obn_ref).astype(BF16)
    oc = cat(ocp_ref, oc_ref, ocn_ref).astype(BF16)
    x_new = (x_ext
             + jnp.dot(a_mix, wout_ref[0:256, :], preferred_element_type=F32)
             + jnp.dot(ob, wout_ref[256:768, :], preferred_element_type=F32)
             + jnp.dot(oc, wout_ref[768:1024, :], preferred_element_type=F32))

    trow = lax.broadcasted_iota(jnp.int32, (rows, 1), 0) + (t0 - HALO)
    t_valid = jnp.logical_and(trow >= 0, trow < seq_len)
    h2_ref[...] = jnp.where(t_valid, _rms_rows(x_new, g2_ref[...]), 0.0).astype(BF16)
    acc_ref[...] = x_new[HALO:HALO + tm, :]

    def body(j, carry):
        h2 = h2_ref[...]
        uv = jnp.dot(h2, wup_ref[0, j], preferred_element_type=F32)
        ug = jnp.dot(h2, wup_ref[1, j], preferred_element_type=F32)
        wv, wg = cfw_ref[0, j], cfw_ref[1, j]
        cv = (uv[HALO - 1:HALO - 1 + tm] * wv[0:1] + uv[HALO:HALO + tm] * wv[1:2]
              + uv[HALO + 1:HALO + 1 + tm] * wv[2:3] + cfb_ref[0, j])
        cg = (ug[HALO - 1:HALO - 1 + tm] * wg[0:1] + ug[HALO:HALO + tm] * wg[1:2]
              + ug[HALO + 1:HALO + 1 + tm] * wg[2:3] + cfb_ref[1, j])
        act = (cv * (cg * _sigmoid(cg))).astype(BF16)
        acc_ref[...] += jnp.dot(act, wdn_ref[j], preferred_element_type=F32)
        return carry

    lax.fori_loop(0, N_FFN_CHUNKS, body, 0)
    o_ref[0] = acc_ref[...]


def _mix_ffn_call(x, a_glu, ob, oc, caw, cab, lng, lnb, w_out, g2, w_up, cfw, cfb, w_dn):
    B, S, _ = x.shape
    tm = TM_FFN
    nt = S // tm

    def halo_specs(width, halo):
        per = tm // halo
        last = S // halo - 1
        main = pl.BlockSpec((1, tm, width), lambda b, i: (b, i, 0))
        prev = pl.BlockSpec((1, halo, width), lambda b, i: (b, jnp.maximum(i * per - 1, 0), 0))
        nxt = pl.BlockSpec((1, halo, width), lambda b, i: (b, jnp.minimum((i + 1) * per, last), 0))
        return [main, prev, nxt]

    def const(a):
        return pl.BlockSpec(a.shape, lambda b, i: (0,) * a.ndim, pipeline_mode=pl.Buffered(1))

    consts = (caw, cab, lng, lnb, w_out, g2, w_up, cfw, cfb, w_dn)
    kern = functools.partial(_mix_ffn_kernel, tm=tm, seq_len=S)
    return pl.pallas_call(
        kern,
        grid=(B, nt),
        in_specs=(halo_specs(D_MODEL, HALO) + halo_specs(CONV_WIDTH, A_HALO)
                  + halo_specs(512, HALO) + halo_specs(256, HALO)
                  + [const(a) for a in consts]),
        out_specs=pl.BlockSpec((1, tm, D_MODEL), lambda b, i: (b, i, 0)),
        out_shape=jax.ShapeDtypeStruct((B, S, D_MODEL), F32),
        scratch_shapes=[pltpu.VMEM((tm + 2 * A_HALO + 16, CONV_WIDTH), F32),
                        pltpu.VMEM((tm + 2 * HALO, D_MODEL), BF16),
                        pltpu.VMEM((tm, D_MODEL), F32)],
        compiler_params=pltpu.CompilerParams(
            dimension_semantics=("parallel", "parallel"), vmem_limit_bytes=VMEM_LIMIT),
        name="mix_ffn",
    )(x, x, x, a_glu, a_glu, a_glu, ob, ob, ob, oc, oc, oc, *consts)


def _period64(parts):
    blk = jnp.concatenate(parts, axis=-1)
    return jnp.concatenate([blk, blk], axis=-1)


def _rope_tables(S):
    pos = jnp.arange(S, dtype=jnp.int32)

    def angles(p, dim, theta):
        inv = theta ** (-jnp.arange(0, dim, 2, dtype=F32) / dim)
        ang = p.astype(F32)[:, None] * inv[None, :]
        return jnp.cos(ang), jnp.sin(ang)

    cp, sp = angles(pos, PARTIAL_ROT_DIM, ROPE_THETA)
    rest = HEAD_DIM - PARTIAL_ROT_DIM
    zr = jnp.zeros((S, rest), F32)
    z8 = jnp.zeros_like(sp)
    tabs_b = (_period64([cp, cp, jnp.ones((S, rest), F32)]),
              _period64([z8, sp, zr]),
              _period64([-sp, z8, zr]))
    cr, sr = angles(pos // GRID_W, AXIAL_HALF, AXIAL_THETA)
    cc, sc = angles(pos % GRID_W, AXIAL_HALF, AXIAL_THETA)
    z16 = jnp.zeros_like(sr)
    tabs_c = (_period64([cr, cr, cc, cc]),
              _period64([z16, sr, z16, sc]),
              _period64([-sr, z16, -sc, z16]))
    return tabs_b + tabs_c


def _layer_params(l, norm1_g, w_in, conv_a_w, conv_a_b, ln_a_g, ln_a_b, qn_b_g, kn_b_g,
                  lam_q1, lam_k1, lam_q2, lam_k2, subln_b_g, qn_c_g, kn_c_g, w_out,
                  norm2_g, w_up, conv_f_w, conv_f_b, w_down):
    w = w_in[l]
    ck = w[:, 2304:2432]
    w_cols = jnp.concatenate(
        [w[:, 0:1536], w[:, 2048:2304], ck[:, 0:64], ck[:, 0:64], ck[:, 64:128], ck[:, 64:128]],
        axis=1).astype(BF16)
    w_vt = jnp.concatenate([w[:, 1536:2048], w[:, 2432:2560]], axis=1).T.astype(BF16)
    scale = HEAD_DIM ** -0.5
    gb = jnp.concatenate([jnp.tile(qn_b_g[l] * scale, 8), jnp.tile(kn_b_g[l], 8)])[None, :]
    gc = jnp.concatenate([jnp.tile(qn_c_g[l] * scale, 4), jnp.tile(kn_c_g[l], 4)])[None, :]
    lam_init = 0.8 - 0.6 * math.exp(-0.3 * l)
    lam_vecs = jnp.stack([lam_q1[l], lam_k1[l], lam_q2[l], lam_k2[l]]).astype(F32)
    lam_consts = jnp.full((SUBLANES, LANES), lam_init, F32)
    wu = w_up[l].astype(BF16).reshape(D_MODEL, 2, N_FFN_CHUNKS, FFN_CHUNK).transpose(1, 2, 0, 3)
    cfw = conv_f_w[l].reshape(3, 2, N_FFN_CHUNKS, FFN_CHUNK).transpose(1, 2, 0, 3)
    cfb = conv_f_b[l].reshape(2, N_FFN_CHUNKS, 1, FFN_CHUNK)
    wd = w_down[l].astype(BF16).reshape(N_FFN_CHUNKS, FFN_CHUNK, D_MODEL)
    return dict(
        g1=norm1_g[l][None, :], w_cols=w_cols, w_vt=w_vt, gb=gb, gc=gc,
        lam_vecs=lam_vecs, lam_consts=lam_consts, subln_col=subln_b_g[l][:, None],
        caw=conv_a_w[l], cab=conv_a_b[l][None, :], lng=ln_a_g[l][None, :], lnb=ln_a_b[l][None, :],
        w_out=w_out[l].astype(BF16), g2=norm2_g[l][None, :], w_up=wu, cfw=cfw, cfb=cfb, w_dn=wd)


def _encoder_layer(x, p, ones, tabs):
    a_glu, qb, kb, vbt, qc, kc, vct = _inproj_call(
        x, p["g1"], p["w_cols"], p["w_vt"], p["gb"], p["gc"], ones, tabs)
    ob = _diff_attn_call(qb, kb, vbt, p["lam_vecs"], p["lam_consts"], p["subln_col"])
    oc = _gqa_attn_call(qc, kc, vct)
    return _mix_ffn_call(x, a_glu, ob, oc, p["caw"], p["cab"], p["lng"], p["lnb"], p["w_out"],
                         p["g2"], p["w_up"], p["cfw"], p["cfb"], p["w_dn"])


def kernel(x_prompt, x_sample, norm1_g, w_in, conv_a_w, conv_a_b, ln_a_g, ln_a_b, qn_b_g, kn_b_g,
           lam_q1, lam_k1, lam_q2, lam_k2, subln_b_g, qn_c_g, kn_c_g, w_out, norm2_g, w_up,
           conv_f_w, conv_f_b, w_down):
    weights = (norm1_g, w_in, conv_a_w, conv_a_b, ln_a_g, ln_a_b, qn_b_g, kn_b_g, lam_q1, lam_k1,
               lam_q2, lam_k2, subln_b_g, qn_c_g, kn_c_g, w_out, norm2_g, w_up, conv_f_w,
               conv_f_b, w_down)
    idx = jnp.arange(256)
    ones = (idx[:, None] // HEAD_DIM == idx[None, :] // HEAD_DIM).astype(BF16)
    tabs_p = _rope_tables(x_prompt.shape[1])
    tabs_s = _rope_tables(x_sample.shape[1])
    y_p, y_s = x_prompt, x_sample
    for l in range(DEPTH):
        p = _layer_params(l, *weights)
        y_p = _encoder_layer(y_p, p, ones, tabs_p)
        y_s = _encoder_layer(y_s, p, ones, tabs_s)
    return (y_p, y_s)
```

```python
import functools
import math

import jax
import jax.numpy as jnp
from jax import lax
from jax.experimental import pallas as pl
from jax.experimental.pallas import tpu as pltpu

F32 = jnp.float32
BF16 = jnp.bfloat16

D_MODEL = 1024
DEPTH = 4
CONV_WIDTH = 256
CONV_K = 31
DIFF_HEADS = 4
HEAD_DIM = 64
DIFF_V_DIM = 128
GQA_KV_HEADS = 2
FFN_HIDDEN = 2816
ROPE_THETA = 500000.0
PARTIAL_ROT_DIM = 16
AXIAL_THETA = 10000.0
AXIAL_HALF = 32
GRID_W = 64
EPS = 1e-6

LANES = 128
SUBLANES = 8
VMEM_LIMIT = 56 * 1024 * 1024

TM_PROJ = 512
TM_FFN = 512
FFN_CHUNK = 256
N_FFN_CHUNKS = FFN_HIDDEN // FFN_CHUNK
HALO = 8
A_HALO = 16

ATTN_UNROLL = 8
TK_SUB = 128
SCORE_LOOKAHEAD = 2
SHIFT_HEADROOM = 80.0
NORM_SLACK = 1.05

NT_DIMS = (((1,), (1,)), ((), ()))


def _rms_rows(x, gain):
    ms = jnp.mean(x * x, axis=-1, keepdims=True)
    return x * lax.rsqrt(ms + EPS) * gain


def _sigmoid(x):
    return 1.0 / (1.0 + jnp.exp(-x))


def _chunk_norm_rope(blk, ones, gain, cos, sin_up, sin_dn, shift):
    sq = blk * blk
    hi = sq.astype(BF16)
    lo = (sq - hi.astype(F32)).astype(BF16)
    ss = (jnp.dot(hi, ones, preferred_element_type=F32)
          + jnp.dot(lo, ones, preferred_element_type=F32))
    y = blk * lax.rsqrt(ss * (1.0 / HEAD_DIM) + EPS) * gain
    pieces = []
    for j in range(blk.shape[1] // LANES):
        yj = y[:, j * LANES:(j + 1) * LANES]
        pieces.append(yj * cos
                      + pltpu.roll(yj, shift, 1) * sin_up
                      + pltpu.roll(yj, LANES - shift, 1) * sin_dn)
    return pieces


def _inproj_kernel(x_ref, g1_ref, w_ref, wvt_ref, gb_ref, gc_ref, ones_ref,
                   cb_ref, sub_ref, sdb_ref, cc_ref, suc_ref, sdc_ref,
                   a_ref, qb_ref, kb_ref, vbt_ref, qc_ref, kc_ref, vct_ref):
    h = _rms_rows(x_ref[0], g1_ref[...]).astype(BF16)

    pa = jnp.dot(h, w_ref[:, 0:512], preferred_element_type=F32)
    a_ref[0] = pa[:, :CONV_WIDTH] * _sigmoid(pa[:, CONV_WIDTH:])

    ones = ones_ref[...]
    cb, sub, sdb = cb_ref[...], sub_ref[...], sdb_ref[...]
    cc, suc, sdc = cc_ref[...], suc_ref[...], sdc_ref[...]

    for blk_i in range(4):
        c0 = 512 + 256 * blk_i
        blk = jnp.dot(h, w_ref[:, c0:c0 + 256], preferred_element_type=F32)
        pieces = _chunk_norm_rope(blk, ones, gb_ref[:, 256 * blk_i:256 * (blk_i + 1)],
                                  cb, sub, sdb, PARTIAL_ROT_DIM // 2)
        dst = qb_ref if blk_i < 2 else kb_ref
        for j, pc in enumerate(pieces):
            lane0 = (256 * blk_i) % 512 + LANES * j
            dst[0, :, lane0:lane0 + LANES] = pc.astype(BF16)

    for blk_i in range(2):
        c0 = 1536 + 256 * blk_i
        blk = jnp.dot(h, w_ref[:, c0:c0 + 256], preferred_element_type=F32)
        pieces = _chunk_norm_rope(blk, ones, gc_ref[:, 256 * blk_i:256 * (blk_i + 1)],
                                  cc, suc, sdc, AXIAL_HALF // 2)
        dst = qc_ref if blk_i == 0 else kc_ref
        for j, pc in enumerate(pieces):
            dst[0, :, LANES * j:LANES * (j + 1)] = pc.astype(BF16)

    vt = lax.dot_general(wvt_ref[...], h, NT_DIMS, preferred_element_type=F32)
    vbt_ref[0, 0] = vt[0:512].astype(BF16)
    vct_ref[0, 0] = vt[512:640].astype(BF16)


def _inproj_call(x, g1, w_cols, w_vt, gb, gc, ones, tabs):
    B, S, _ = x.shape
    tm = TM_PROJ
    nt = S // tm
    tok = lambda w: pl.BlockSpec((1, tm, w), lambda b, i: (b, i, 0))
    full = lambda a: pl.BlockSpec(a.shape, lambda b, i: (0,) * a.ndim)
    tab = pl.BlockSpec((tm, LANES), lambda b, i: (i, 0))
    out_shape = (
        jax.ShapeDtypeStruct((B, S, CONV_WIDTH), F32),
        jax.ShapeDtypeStruct((B, S, 512), BF16),
        jax.ShapeDtypeStruct((B, S, 512), BF16),
        jax.ShapeDtypeStruct((B, nt, 512, tm), BF16),
        jax.ShapeDtypeStruct((B, S, 256), BF16),
        jax.ShapeDtypeStruct((B, S, 256), BF16),
        jax.ShapeDtypeStruct((B, nt, 128, tm), BF16),
    )
    out_specs = (
        tok(CONV_WIDTH), tok(512), tok(512),
        pl.BlockSpec((1, 1, 512, tm), lambda b, i: (b, i, 0, 0)),
        tok(256), tok(256),
        pl.BlockSpec((1, 1, 128, tm), lambda b, i: (b, i, 0, 0)),
    )
    return pl.pallas_call(
        _inproj_kernel,
        grid=(B, nt),
        in_specs=[tok(D_MODEL), full(g1), full(w_cols), full(w_vt), full(gb), full(gc), full(ones)]
                 + [tab] * 6,
        out_specs=out_specs,
        out_shape=out_shape,
        compiler_params=pltpu.CompilerParams(
            dimension_semantics=("parallel", "parallel"), vmem_limit_bytes=VMEM_LIMIT),
        name="inproj",
    )(x, g1, w_cols, w_vt, gb, gc, ones, *tabs)


def _flash_core(q_ref, k_ref, vt_ref, qop_ref, kmax_ref, m_ref, l_ref, acc_ref, tq, tk, n_chunks):
    r_id = lax.broadcasted_iota(jnp.int32, (LANES, LANES), 0) // HEAD_DIM
    c_id = lax.broadcasted_iota(jnp.int32, (LANES, LANES), 1) // HEAD_DIM
    half_ones = (r_id == c_id).astype(F32)

    @pl.when(pl.program_id(2) == 0)
    def _key_norm_bound():
        def kb(c, mx):
            k = k_ref[0, pl.ds(pl.multiple_of(c * tk, tk), tk), :].astype(F32)
            n2 = jnp.dot(k * k, half_ones, preferred_element_type=F32)
            return jnp.maximum(mx, jnp.max(n2, axis=0, keepdims=True))
        kmax_ref[...] = lax.fori_loop(0, n_chunks, kb, jnp.zeros((1, LANES), F32))

    q = q_ref[0].astype(F32)
    lane = lax.broadcasted_iota(jnp.int32, q.shape, 1)
    qop_ref[0:tq, :] = jnp.where(lane < HEAD_DIM, q, 0.0).astype(BF16)
    qop_ref[tq:2 * tq, :] = jnp.where(lane >= HEAD_DIM, q, 0.0).astype(BF16)

    def chunk(c):
        k = k_ref[0, pl.ds(pl.multiple_of(c * tk, tk), tk), :]
        return lax.dot_general(k, qop_ref[...], NT_DIMS, preferred_element_type=F32)

    def weigh(c, p):
        return jnp.dot(vt_ref[0, c], p.astype(BF16), preferred_element_type=F32)

    s0 = lax.dot_general(k_ref[0, 0:TK_SUB, :], qop_ref[...], NT_DIMS, preferred_element_type=F32)
    m0 = jnp.max(s0, axis=0, keepdims=True)
    m_ref[...] = m0
    l_ref[...] = jnp.zeros(l_ref.shape, F32)
    acc_ref[...] = jnp.zeros(acc_ref.shape, F32)

    q2 = jnp.max(jnp.max(jnp.dot(q * q, half_ones, preferred_element_type=F32),
                         axis=0, keepdims=True), axis=1, keepdims=True)
    k2 = jnp.max(kmax_ref[...], axis=1, keepdims=True)
    room = SHIFT_HEADROOM + jnp.min(m0, axis=1, keepdims=True)
    safe = jnp.logical_and(room > 0.0, q2 * k2 * NORM_SLACK <= room * room)
    n_safe = jnp.sum(safe.astype(jnp.int32))

    n_sub = tk // TK_SUB

    def frozen_block(c0, n_blk):
        m = m_ref[...]
        subs = [(ci, j) for ci in range(n_blk) for j in range(n_sub)]

        def sub_scores(ci, j):
            row0 = pl.multiple_of((c0 + ci) * tk + j * TK_SUB, TK_SUB)
            k = k_ref[0, pl.ds(row0, TK_SUB), :]
            return lax.dot_general(k, qop_ref[...], NT_DIMS, preferred_element_type=F32)

        pending = {t: sub_scores(*subs[t]) for t in range(min(SCORE_LOOKAHEAD, len(subs)))}
        l_add = jnp.zeros(m.shape, F32)
        ps = []
        for t, (ci, j) in enumerate(subs):
            if t + SCORE_LOOKAHEAD < len(subs):
                pending[t + SCORE_LOOKAHEAD] = sub_scores(*subs[t + SCORE_LOOKAHEAD])
            p = jnp.exp2(pending.pop(t) - m)
            l_add = l_add + jnp.sum(p, axis=0, keepdims=True)
            ps.append(p.astype(BF16))
            if j == n_sub - 1:
                acc_ref[...] += weigh(c0 + ci, jnp.concatenate(ps, axis=0))
                ps = []
        l_ref[...] += l_add

    def online_body(c, carry):
        s = chunk(c)
        m_old = m_ref[...]
        m_new = jnp.maximum(m_old, jnp.max(s, axis=0, keepdims=True))
        alpha = jnp.exp2(m_old - m_new)
        p = jnp.exp2(s - m_new)
        l_ref[...] = alpha * l_ref[...] + jnp.sum(p, axis=0, keepdims=True)
        acc_ref[...] = alpha * acc_ref[...] + weigh(c, p)
        m_ref[...] = m_new
        return carry

    @pl.when(n_safe > 0)
    def _frozen_shift():
        n_full, n_tail = divmod(n_chunks, ATTN_UNROLL)

        def full_block(i, carry):
            frozen_block(i * ATTN_UNROLL, ATTN_UNROLL)
            return carry

        if n_full > 1:
            lax.fori_loop(0, n_full, full_block, 0)
        elif n_full == 1:
            frozen_block(0, ATTN_UNROLL)
        if n_tail > 0:
            frozen_block(n_full * ATTN_UNROLL, n_tail)

    @pl.when(n_safe == 0)
    def _online_shift():
        lax.fori_loop(0, n_chunks, online_body, 0)

    return acc_ref[...] * (1.0 / l_ref[...])


def _run_flash(q_ref, k_ref, vt_ref, scratch, tq, tk, n_chunks):
    qop_ref, kmax_ref, m_ref, l_ref, acc_ref = scratch
    return _flash_core(q_ref, k_ref, vt_ref, qop_ref, kmax_ref, m_ref, l_ref, acc_ref, tq, tk, n_chunks)


def _diff_attn_kernel(q_ref, k_ref, vt_ref, lam_ref, lamc_ref, g_ref, o_ref, *scratch, tq, tk, n_chunks):
    o = _run_flash(q_ref, k_ref, vt_ref, scratch, tq, tk, n_chunks)
    lam_init = lamc_ref[0:1, 0:1]
    lam = (jnp.exp(jnp.sum(lam_ref[0:1, :] * lam_ref[1:2, :], axis=-1, keepdims=True))
           - jnp.exp(jnp.sum(lam_ref[2:3, :] * lam_ref[3:4, :], axis=-1, keepdims=True))
           + lam_init)
    od = o[:, 0:tq] - lam * o[:, tq:2 * tq]
    ms = jnp.mean(od * od, axis=0, keepdims=True)
    y = od * lax.rsqrt(ms + EPS) * g_ref[...] * (1.0 - lam_init)
    o_ref[0] = y.T


def _gqa_attn_kernel(q_ref, k_ref, vt_ref, o_ref, *scratch, tq, tk, n_chunks):
    o = _run_flash(q_ref, k_ref, vt_ref, scratch, tq, tk, n_chunks)
    o_ref[0] = jnp.concatenate([o[:, 0:tq], o[:, tq:2 * tq]], axis=0).T


def _attn_tq(S):
    return 256


def _attn_scratch(tq, tk, dv):
    row = pltpu.VMEM((1, 2 * tq), F32)
    return [pltpu.VMEM((2 * tq, LANES), BF16), pltpu.VMEM((1, LANES), F32), row, row,
            pltpu.VMEM((dv, 2 * tq), F32)]


def _diff_attn_call(qb, kb, vbt, lam_vecs, lam_consts, subln_col):
    B, S, _ = qb.shape
    tq, tk = _attn_tq(S), TM_PROJ
    nc = S // tk
    kern = functools.partial(_diff_attn_kernel, tq=tq, tk=tk, n_chunks=nc)
    return pl.pallas_call(
        kern,
        grid=(B, DIFF_HEADS, S // tq),
        in_specs=[
            pl.BlockSpec((1, tq, LANES), lambda b, h, i: (b, i, h)),
            pl.BlockSpec((1, S, LANES), lambda b, h, i: (b, 0, h)),
            pl.BlockSpec((1, nc, DIFF_V_DIM, tk), lambda b, h, i: (b, 0, h, 0)),
            pl.BlockSpec(lam_vecs.shape, lambda b, h, i: (0, 0)),
            pl.BlockSpec(lam_consts.shape, lambda b, h, i: (0, 0)),
            pl.BlockSpec(subln_col.shape, lambda b, h, i: (0, 0)),
        ],
        out_specs=pl.BlockSpec((1, tq, LANES), lambda b, h, i: (b, i, h)),
        out_shape=jax.ShapeDtypeStruct((B, S, DIFF_HEADS * DIFF_V_DIM), F32),
        scratch_shapes=_attn_scratch(tq, tk, DIFF_V_DIM),
        compiler_params=pltpu.CompilerParams(
            dimension_semantics=("parallel", "parallel", "arbitrary"), vmem_limit_bytes=VMEM_LIMIT),
        name="diff_attn",
    )(qb, kb, vbt, lam_vecs, lam_consts, subln_col)


def _gqa_attn_call(qc, kc, vct):
    B, S, _ = qc.shape
    tq, tk = _attn_tq(S), TM_PROJ
    nc = S // tk
    kern = functools.partial(_gqa_attn_kernel, tq=tq, tk=tk, n_chunks=nc)
    return pl.pallas_call(
        kern,
        grid=(B, GQA_KV_HEADS, S // tq),
        in_specs=[
            pl.BlockSpec((1, tq, LANES), lambda b, g, i: (b, i, g)),
            pl.BlockSpec((1, S, LANES), lambda b, g, i: (b, 0, g)),
            pl.BlockSpec((1, nc, HEAD_DIM, tk), lambda b, g, i: (b, 0, g, 0)),
        ],
        out_specs=pl.BlockSpec((1, tq, LANES), lambda b, g, i: (b, i, g)),
        out_shape=jax.ShapeDtypeStruct((B, S, 2 * LANES), F32),
        scratch_shapes=_attn_scratch(tq, tk, HEAD_DIM),
        compiler_params=pltpu.CompilerParams(
            dimension_semantics=("parallel", "parallel", "arbitrary"), vmem_limit_bytes=VMEM_LIMIT),
        name="gqa_attn",
    )(qc, kc, vct)


def _mix_ffn_kernel(x_ref, xp_ref, xn_ref, a_ref, ap_ref, an_ref,
                    ob_ref, obp_ref, obn_ref, oc_ref, ocp_ref, ocn_ref,
                    caw_ref, cab_ref, lng_ref, lnb_ref, wout_ref, g2_ref,
                    wup_ref, cfw_ref, cfb_ref, wdn_ref,
                    o_ref, as_ref, h2_ref, acc_ref, *, tm, seq_len):
    rows = tm + 2 * HALO
    t0 = pl.program_id(1) * tm

    as_ref[0:8, :] = jnp.zeros((8, CONV_WIDTH), F32)
    as_ref[8:8 + A_HALO, :] = ap_ref[0]
    as_ref[8 + A_HALO:8 + A_HALO + tm, :] = a_ref[0]
    as_ref[8 + A_HALO + tm:8 + 2 * A_HALO + tm, :] = an_ref[0]
    as_ref[8 + 2 * A_HALO + tm:16 + 2 * A_HALO + tm, :] = jnp.zeros((8, CONV_WIDTH), F32)
    arow = lax.broadcasted_iota(jnp.int32, (tm + 2 * A_HALO + 16, 1), 0) + (t0 - A_HALO - 8)
    a_valid = jnp.logical_and(arow >= 0, arow < seq_len)
    as_ref[...] = jnp.where(a_valid, as_ref[...], 0.0)

    conv = jnp.zeros((rows, CONV_WIDTH), F32) + cab_ref[...]
    for r in range(SUBLANES):
        shifted = as_ref[r:r + rows + 3 * SUBLANES, :]
        for q in range(4):
            k = SUBLANES * q + r - 1
            if 0 <= k < CONV_K:
                conv = conv + shifted[SUBLANES * q:SUBLANES * q + rows] * caw_ref[k:k + 1, :]
    mu = jnp.mean(conv, axis=-1, keepdims=True)
    xc = conv - mu
    var = jnp.mean(xc * xc, axis=-1, keepdims=True)
    ln = xc * lax.rsqrt(var + EPS) * lng_ref[...] + lnb_ref[...]
    a_mix = (ln * _sigmoid(ln)).astype(BF16)

    cat = lambda p, c, n: jnp.concatenate([p[0], c[0], n[0]], axis=0)
    x_ext = cat(xp_ref, x_ref, xn_ref)
    ob = cat(obp_ref, ob_ref, obn_ref).astype(BF16)
    oc = cat(ocp_ref, oc_ref, ocn_ref).astype(BF16)
    x_new = (x_ext
             + jnp.dot(a_mix, wout_ref[0:256, :], preferred_element_type=F32)
             + jnp.dot(ob, wout_ref[256:768, :], preferred_element_type=F32)
             + jnp.dot(oc, wout_ref[768:1024, :], preferred_element_type=F32))

    trow = lax.broadcasted_iota(jnp.int32, (rows, 1), 0) + (t0 - HALO)
    t_valid = jnp.logical_and(trow >= 0, trow < seq_len)
    h2_ref[...] = jnp.where(t_valid, _rms_rows(x_new, g2_ref[...]), 0.0).astype(BF16)
    acc_ref[...] = x_new[HALO:HALO + tm, :]

    def up_proj(j, half):
        return jnp.dot(h2_ref[...], wup_ref[half, j], preferred_element_type=F32)

    def conv3(u, w, b):
        prev = pltpu.roll(u, 1, 0)[HALO:HALO + tm]
        nxt = pltpu.roll(u, rows - 1, 0)[HALO:HALO + tm]
        return prev * w[0:1] + u[HALO:HALO + tm] * w[1:2] + nxt * w[2:3] + b

    def body(j, carry):
        cv = conv3(up_proj(j, 0), cfw_ref[0, j], cfb_ref[0, j])
        cg = conv3(up_proj(j, 1), cfw_ref[1, j], cfb_ref[1, j])
        act = (cv * (cg * _sigmoid(cg))).astype(BF16)
        acc_ref[...] += jnp.dot(act, wdn_ref[j], preferred_element_type=F32)
        return carry

    lax.fori_loop(0, N_FFN_CHUNKS, body, 0)
    o_ref[0] = acc_ref[...]


def _mix_ffn_call(x, a_glu, ob, oc, caw, cab, lng, lnb, w_out, g2, w_up, cfw, cfb, w_dn):
    B, S, _ = x.shape
    tm = TM_FFN
    nt = S // tm

    def halo_specs(width, halo):
        per = tm // halo
        last = S // halo - 1
        main = pl.BlockSpec((1, tm, width), lambda b, i: (b, i, 0))
        prev = pl.BlockSpec((1, halo, width), lambda b, i: (b, jnp.maximum(i * per - 1, 0), 0))
        nxt = pl.BlockSpec((1, halo, width), lambda b, i: (b, jnp.minimum((i + 1) * per, last), 0))
        return [main, prev, nxt]

    def const(a):
        return pl.BlockSpec(a.shape, lambda b, i: (0,) * a.ndim, pipeline_mode=pl.Buffered(1))

    consts = (caw, cab, lng, lnb, w_out, g2, w_up, cfw, cfb, w_dn)
    kern = functools.partial(_mix_ffn_kernel, tm=tm, seq_len=S)
    return pl.pallas_call(
        kern,
        grid=(B, nt),
        in_specs=(halo_specs(D_MODEL, HALO) + halo_specs(CONV_WIDTH, A_HALO)
                  + halo_specs(512, HALO) + halo_specs(256, HALO)
                  + [const(a) for a in consts]),
        out_specs=pl.BlockSpec((1, tm, D_MODEL), lambda b, i: (b, i, 0)),
        out_shape=jax.ShapeDtypeStruct((B, S, D_MODEL), F32),
        scratch_shapes=[pltpu.VMEM((tm + 2 * A_HALO + 16, CONV_WIDTH), F32),
                        pltpu.VMEM((tm + 2 * HALO, D_MODEL), BF16),
                        pltpu.VMEM((tm, D_MODEL), F32)],
        compiler_params=pltpu.CompilerParams(
            dimension_semantics=("parallel", "parallel"), vmem_limit_bytes=VMEM_LIMIT),
        name="mix_ffn",
    )(x, x, x, a_glu, a_glu, a_glu, ob, ob, ob, oc, oc, oc, *consts)


def _period64(parts):
    blk = jnp.concatenate(parts, axis=-1)
    return jnp.concatenate([blk, blk], axis=-1)


def _rope_tables(S):
    pos = jnp.arange(S, dtype=jnp.int32)

    def angles(p, dim, theta):
        inv = theta ** (-jnp.arange(0, dim, 2, dtype=F32) / dim)
        ang = p.astype(F32)[:, None] * inv[None, :]
        return jnp.cos(ang), jnp.sin(ang)

    cp, sp = angles(pos, PARTIAL_ROT_DIM, ROPE_THETA)
    rest = HEAD_DIM - PARTIAL_ROT_DIM
    zr = jnp.zeros((S, rest), F32)
    z8 = jnp.zeros_like(sp)
    tabs_b = (_period64([cp, cp, jnp.ones((S, rest), F32)]),
              _period64([z8, sp, zr]),
              _period64([-sp, z8, zr]))
    cr, sr = angles(pos // GRID_W, AXIAL_HALF, AXIAL_THETA)
    cc, sc = angles(pos % GRID_W, AXIAL_HALF, AXIAL_THETA)
    z16 = jnp.zeros_like(sr)
    tabs_c = (_period64([cr, cr, cc, cc]),
              _period64([z16, sr, z16, sc]),
              _period64([-sr, z16, -sc, z16]))
    return tabs_b + tabs_c


def _layer_params(l, norm1_g, w_in, conv_a_w, conv_a_b, ln_a_g, ln_a_b, qn_b_g, kn_b_g,
                  lam_q1, lam_k1, lam_q2, lam_k2, subln_b_g, qn_c_g, kn_c_g, w_out,
                  norm2_g, w_up, conv_f_w, conv_f_b, w_down):
    w = w_in[l]
    ck = w[:, 2304:2432]
    w_cols = jnp.concatenate(
        [w[:, 0:1536], w[:, 2048:2304], ck[:, 0:64], ck[:, 0:64], ck[:, 64:128], ck[:, 64:128]],
        axis=1).astype(BF16)
    w_vt = jnp.concatenate([w[:, 1536:2048], w[:, 2432:2560]], axis=1).T.astype(BF16)
    scale = HEAD_DIM ** -0.5 * math.log2(math.e)
    gb = jnp.concatenate([jnp.tile(qn_b_g[l] * scale, 8), jnp.tile(kn_b_g[l], 8)])[None, :]
    gc = jnp.concatenate([jnp.tile(qn_c_g[l] * scale, 4), jnp.tile(kn_c_g[l], 4)])[None, :]
    lam_init = 0.8 - 0.6 * math.exp(-0.3 * l)
    lam_vecs = jnp.stack([lam_q1[l], lam_k1[l], lam_q2[l], lam_k2[l]]).astype(F32)
    lam_consts = jnp.full((SUBLANES, LANES), lam_init, F32)
    wu = w_up[l].astype(BF16).reshape(D_MODEL, 2, N_FFN_CHUNKS, FFN_CHUNK).transpose(1, 2, 0, 3)
    cfw = conv_f_w[l].reshape(3, 2, N_FFN_CHUNKS, FFN_CHUNK).transpose(1, 2, 0, 3)
    cfb = conv_f_b[l].reshape(2, N_FFN_CHUNKS, 1, FFN_CHUNK)
    wd = w_down[l].astype(BF16).reshape(N_FFN_CHUNKS, FFN_CHUNK, D_MODEL)
    return dict(
        g1=norm1_g[l][None, :], w_cols=w_cols, w_vt=w_vt, gb=gb, gc=gc,
        lam_vecs=lam_vecs, lam_consts=lam_consts, subln_col=subln_b_g[l][:, None],
        caw=conv_a_w[l], cab=conv_a_b[l][None, :], lng=ln_a_g[l][None, :], lnb=ln_a_b[l][None, :],
        w_out=w_out[l].astype(BF16), g2=norm2_g[l][None, :], w_up=wu, cfw=cfw, cfb=cfb, w_dn=wd)


def _encoder_layer(x, p, ones, tabs):
    a_glu, qb, kb, vbt, qc, kc, vct = _inproj_call(
        x, p["g1"], p["w_cols"], p["w_vt"], p["gb"], p["gc"], ones, tabs)
    ob = _diff_attn_call(qb, kb, vbt, p["lam_vecs"], p["lam_consts"], p["subln_col"])
    oc = _gqa_attn_call(qc, kc, vct)
    return _mix_ffn_call(x, a_glu, ob, oc, p["caw"], p["cab"], p["lng"], p["lnb"], p["w_out"],
                         p["g2"], p["w_up"], p["cfw"], p["cfb"], p["w_dn"])


def kernel(x_prompt, x_sample, norm1_g, w_in, conv_a_w, conv_a_b, ln_a_g, ln_a_b, qn_b_g, kn_b_g,
           lam_q1, lam_k1, lam_q2, lam_k2, subln_b_g, qn_c_g, kn_c_g, w_out, norm2_g, w_up,
           conv_f_w, conv_f_b, w_down):
    weights = (norm1_g, w_in, conv_a_w, conv_a_b, ln_a_g, ln_a_b, qn_b_g, kn_b_g, lam_q1, lam_k1,
               lam_q2, lam_k2, subln_b_g, qn_c_g, kn_c_g, w_out, norm2_g, w_up, conv_f_w,
               conv_f_b, w_down)
    idx = jnp.arange(256)
    ones = (idx[:, None] // HEAD_DIM == idx[None, :] // HEAD_DIM).astype(BF16)
    tabs_p = _rope_tables(x_prompt.shape[1])
    tabs_s = _rope_tables(x_sample.shape[1])
    y_p, y_s = x_prompt, x_sample
    for l in range(DEPTH):
        p = _layer_params(l, *weights)
        y_p = _encoder_layer(y_p, p, ones, tabs_p)
        y_s = _encoder_layer(y_s, p, ones, tabs_s)
    return (y_p, y_s)
```

```python
import functools
import math

import jax
import jax.numpy as jnp
from jax import lax
from jax.experimental import pallas as pl
from jax.experimental.pallas import tpu as pltpu

F32 = jnp.float32
BF16 = jnp.bfloat16

D_MODEL = 1024
DEPTH = 4
CONV_WIDTH = 256
CONV_K = 31
DIFF_HEADS = 4
HEAD_DIM = 64
DIFF_V_DIM = 128
GQA_KV_HEADS = 2
FFN_HIDDEN = 2816
ROPE_THETA = 500000.0
PARTIAL_ROT_DIM = 16
AXIAL_THETA = 10000.0
AXIAL_HALF = 32
GRID_W = 64
EPS = 1e-6

LANES = 128
SUBLANES = 8
VMEM_LIMIT = 56 * 1024 * 1024

TM_PROJ = 512
TM_FFN = 512
FFN_CHUNK = 256
N_FFN_CHUNKS = FFN_HIDDEN // FFN_CHUNK
HALO = 8
A_HALO = 16

ATTN_UNROLL = 8
TK_SUB = 128
SCORE_LOOKAHEAD = 2
SHIFT_HEADROOM = 80.0
NORM_SLACK = 1.05

NT_DIMS = (((1,), (1,)), ((), ()))


def _rms_rows(x, gain):
    ms = jnp.mean(x * x, axis=-1, keepdims=True)
    return x * lax.rsqrt(ms + EPS) * gain


def _sigmoid(x):
    return 1.0 / (1.0 + jnp.exp(-x))


def _chunk_norm_rope(blk, ones, gain, cos, sin_up, sin_dn, shift):
    sq = blk * blk
    hi = sq.astype(BF16)
    lo = (sq - hi.astype(F32)).astype(BF16)
    ss = (jnp.dot(hi, ones, preferred_element_type=F32)
          + jnp.dot(lo, ones, preferred_element_type=F32))
    y = blk * lax.rsqrt(ss * (1.0 / HEAD_DIM) + EPS) * gain
    pieces = []
    for j in range(blk.shape[1] // LANES):
        yj = y[:, j * LANES:(j + 1) * LANES]
        pieces.append(yj * cos
                      + pltpu.roll(yj, shift, 1) * sin_up
                      + pltpu.roll(yj, LANES - shift, 1) * sin_dn)
    return pieces


def _inproj_kernel(x_ref, g1_ref, w_ref, wvt_ref, gb_ref, gc_ref, ones_ref,
                   cb_ref, sub_ref, sdb_ref, cc_ref, suc_ref, sdc_ref,
                   a_ref, qb_ref, kb_ref, vbt_ref, qc_ref, kc_ref, vct_ref):
    h = _rms_rows(x_ref[0], g1_ref[...]).astype(BF16)

    pa = jnp.dot(h, w_ref[:, 0:512], preferred_element_type=F32)
    a_ref[0] = pa[:, :CONV_WIDTH] * _sigmoid(pa[:, CONV_WIDTH:])

    ones = ones_ref[...]
    cb, sub, sdb = cb_ref[...], sub_ref[...], sdb_ref[...]
    cc, suc, sdc = cc_ref[...], suc_ref[...], sdc_ref[...]

    for blk_i in range(4):
        c0 = 512 + 256 * blk_i
        blk = jnp.dot(h, w_ref[:, c0:c0 + 256], preferred_element_type=F32)
        pieces = _chunk_norm_rope(blk, ones, gb_ref[:, 256 * blk_i:256 * (blk_i + 1)],
                                  cb, sub, sdb, PARTIAL_ROT_DIM // 2)
        dst = qb_ref if blk_i < 2 else kb_ref
        for j, pc in enumerate(pieces):
            lane0 = (256 * blk_i) % 512 + LANES * j
            dst[0, :, lane0:lane0 + LANES] = pc.astype(BF16)

    for blk_i in range(2):
        c0 = 1536 + 256 * blk_i
        blk = jnp.dot(h, w_ref[:, c0:c0 + 256], preferred_element_type=F32)
        pieces = _chunk_norm_rope(blk, ones, gc_ref[:, 256 * blk_i:256 * (blk_i + 1)],
                                  cc, suc, sdc, AXIAL_HALF // 2)
        dst = qc_ref if blk_i == 0 else kc_ref
        for j, pc in enumerate(pieces):
            dst[0, :, LANES * j:LANES * (j + 1)] = pc.astype(BF16)

    vt = lax.dot_general(wvt_ref[...], h, NT_DIMS, preferred_element_type=F32)
    vbt_ref[0, 0] = vt[0:512].astype(BF16)
    vct_ref[0, 0] = vt[512:640].astype(BF16)


def _inproj_call(x, g1, w_cols, w_vt, gb, gc, ones, tabs):
    B, S, _ = x.shape
    tm = TM_PROJ
    nt = S // tm
    tok = lambda w: pl.BlockSpec((1, tm, w), lambda b, i: (b, i, 0))
    full = lambda a: pl.BlockSpec(a.shape, lambda b, i: (0,) * a.ndim)
    tab = pl.BlockSpec((tm, LANES), lambda b, i: (i, 0))
    out_shape = (
        jax.ShapeDtypeStruct((B, S, CONV_WIDTH), F32),
        jax.ShapeDtypeStruct((B, S, 512), BF16),
        jax.ShapeDtypeStruct((B, S, 512), BF16),
        jax.ShapeDtypeStruct((B, nt, 512, tm), BF16),
        jax.ShapeDtypeStruct((B, S, 256), BF16),
        jax.ShapeDtypeStruct((B, S, 256), BF16),
        jax.ShapeDtypeStruct((B, nt, 128, tm), BF16),
    )
    out_specs = (
        tok(CONV_WIDTH), tok(512), tok(512),
        pl.BlockSpec((1, 1, 512, tm), lambda b, i: (b, i, 0, 0)),
        tok(256), tok(256),
        pl.BlockSpec((1, 1, 128, tm), lambda b, i: (b, i, 0, 0)),
    )
    return pl.pallas_call(
        _inproj_kernel,
        grid=(B, nt),
        in_specs=[tok(D_MODEL), full(g1), full(w_cols), full(w_vt), full(gb), full(gc), full(ones)]
                 + [tab] * 6,
        out_specs=out_specs,
        out_shape=out_shape,
        compiler_params=pltpu.CompilerParams(
            dimension_semantics=("parallel", "parallel"), vmem_limit_bytes=VMEM_LIMIT),
        name="inproj",
    )(x, g1, w_cols, w_vt, gb, gc, ones, *tabs)


def _flash_core(q_ref, k_ref, vt_ref, qop_ref, kmax_ref, m_ref, l_ref, acc_ref, tq, tk, n_chunks):
    r_id = lax.broadcasted_iota(jnp.int32, (LANES, LANES), 0) // HEAD_DIM
    c_id = lax.broadcasted_iota(jnp.int32, (LANES, LANES), 1) // HEAD_DIM
    half_ones = (r_id == c_id).astype(F32)

    @pl.when(pl.program_id(2) == 0)
    def _key_norm_bound():
        def kb(c, mx):
            k = k_ref[0, pl.ds(pl.multiple_of(c * tk, tk), tk), :].astype(F32)
            n2 = jnp.dot(k * k, half_ones, preferred_element_type=F32)
            return jnp.maximum(mx, jnp.max(n2, axis=0, keepdims=True))
        kmax_ref[...] = lax.fori_loop(0, n_chunks, kb, jnp.zeros((1, LANES), F32))

    q = q_ref[0].astype(F32)
    lane = lax.broadcasted_iota(jnp.int32, q.shape, 1)
    qop_ref[0:tq, :] = jnp.where(lane < HEAD_DIM, q, 0.0).astype(BF16)
    qop_ref[tq:2 * tq, :] = jnp.where(lane >= HEAD_DIM, q, 0.0).astype(BF16)

    def chunk(c):
        k = k_ref[0, pl.ds(pl.multiple_of(c * tk, tk), tk), :]
        return lax.dot_general(k, qop_ref[...], NT_DIMS, preferred_element_type=F32)

    def weigh(c, p):
        return jnp.dot(vt_ref[0, c], p.astype(BF16), preferred_element_type=F32)

    s0 = lax.dot_general(k_ref[0, 0:TK_SUB, :], qop_ref[...], NT_DIMS, preferred_element_type=F32)
    m0 = jnp.max(s0, axis=0, keepdims=True)
    m_ref[...] = m0
    l_ref[...] = jnp.zeros(l_ref.shape, F32)
    acc_ref[...] = jnp.zeros(acc_ref.shape, F32)

    q2 = jnp.max(jnp.max(jnp.dot(q * q, half_ones, preferred_element_type=F32),
                         axis=0, keepdims=True), axis=1, keepdims=True)
    k2 = jnp.max(kmax_ref[...], axis=1, keepdims=True)
    room = SHIFT_HEADROOM + jnp.min(m0, axis=1, keepdims=True)
    safe = jnp.logical_and(room > 0.0, q2 * k2 * NORM_SLACK <= room * room)
    n_safe = jnp.sum(safe.astype(jnp.int32))

    n_sub = tk // TK_SUB

    def frozen_block(c0, n_blk):
        m = m_ref[...]
        subs = [(ci, j) for ci in range(n_blk) for j in range(n_sub)]

        def sub_scores(ci, j):
            row0 = pl.multiple_of((c0 + ci) * tk + j * TK_SUB, TK_SUB)
            k = k_ref[0, pl.ds(row0, TK_SUB), :]
            return lax.dot_general(k, qop_ref[...], NT_DIMS, preferred_element_type=F32)

        pending = {t: sub_scores(*subs[t]) for t in range(min(SCORE_LOOKAHEAD, len(subs)))}
        l_add = jnp.zeros(m.shape, F32)
        ps = []
        for t, (ci, j) in enumerate(subs):
            if t + SCORE_LOOKAHEAD < len(subs):
                pending[t + SCORE_LOOKAHEAD] = sub_scores(*subs[t + SCORE_LOOKAHEAD])
            p = jnp.exp2(pending.pop(t) - m)
            l_add = l_add + jnp.sum(p, axis=0, keepdims=True)
            ps.append(p.astype(BF16))
            if j == n_sub - 1:
                acc_ref[...] += weigh(c0 + ci, jnp.concatenate(ps, axis=0))
                ps = []
        l_ref[...] += l_add

    def online_body(c, carry):
        s = chunk(c)
        m_old = m_ref[...]
        m_new = jnp.maximum(m_old, jnp.max(s, axis=0, keepdims=True))
        alpha = jnp.exp2(m_old - m_new)
        p = jnp.exp2(s - m_new)
        l_ref[...] = alpha * l_ref[...] + jnp.sum(p, axis=0, keepdims=True)
        acc_ref[...] = alpha * acc_ref[...] + weigh(c, p)
        m_ref[...] = m_new
        return carry

    @pl.when(n_safe > 0)
    def _frozen_shift():
        n_full, n_tail = divmod(n_chunks, ATTN_UNROLL)

        def full_block(i, carry):
            frozen_block(i * ATTN_UNROLL, ATTN_UNROLL)
            return carry

        if n_full > 1:
            lax.fori_loop(0, n_full, full_block, 0)
        elif n_full == 1:
            frozen_block(0, ATTN_UNROLL)
        if n_tail > 0:
            frozen_block(n_full * ATTN_UNROLL, n_tail)

    @pl.when(n_safe == 0)
    def _online_shift():
        lax.fori_loop(0, n_chunks, online_body, 0)

    return acc_ref[...] * (1.0 / l_ref[...])


def _run_flash(q_ref, k_ref, vt_ref, scratch, tq, tk, n_chunks):
    qop_ref, kmax_ref, m_ref, l_ref, acc_ref = scratch
    return _flash_core(q_ref, k_ref, vt_ref, qop_ref, kmax_ref, m_ref, l_ref, acc_ref, tq, tk, n_chunks)


def _diff_attn_kernel(q_ref, k_ref, vt_ref, lam_ref, lamc_ref, g_ref, o_ref, *scratch, tq, tk, n_chunks):
    o = _run_flash(q_ref, k_ref, vt_ref, scratch, tq, tk, n_chunks)
    lam_init = lamc_ref[0:1, 0:1]
    lam = (jnp.exp(jnp.sum(lam_ref[0:1, :] * lam_ref[1:2, :], axis=-1, keepdims=True))
           - jnp.exp(jnp.sum(lam_ref[2:3, :] * lam_ref[3:4, :], axis=-1, keepdims=True))
           + lam_init)
    od = o[:, 0:tq] - lam * o[:, tq:2 * tq]
    ms = jnp.mean(od * od, axis=0, keepdims=True)
    y = od * lax.rsqrt(ms + EPS) * g_ref[...] * (1.0 - lam_init)
    o_ref[0] = y.T


def _gqa_attn_kernel(q_ref, k_ref, vt_ref, o_ref, *scratch, tq, tk, n_chunks):
    o = _run_flash(q_ref, k_ref, vt_ref, scratch, tq, tk, n_chunks)
    o_ref[0] = jnp.concatenate([o[:, 0:tq], o[:, tq:2 * tq]], axis=0).T


def _attn_tq(S):
    return 512


def _attn_scratch(tq, tk, dv):
    row = pltpu.VMEM((1, 2 * tq), F32)
    return [pltpu.VMEM((2 * tq, LANES), BF16), pltpu.VMEM((1, LANES), F32), row, row,
            pltpu.VMEM((dv, 2 * tq), F32)]


def _diff_attn_call(qb, kb, vbt, lam_vecs, lam_consts, subln_col):
    B, S, _ = qb.shape
    tq, tk = _attn_tq(S), TM_PROJ
    nc = S // tk
    kern = functools.partial(_diff_attn_kernel, tq=tq, tk=tk, n_chunks=nc)
    return pl.pallas_call(
        kern,
        grid=(B, DIFF_HEADS, S // tq),
        in_specs=[
            pl.BlockSpec((1, tq, LANES), lambda b, h, i: (b, i, h)),
            pl.BlockSpec((1, S, LANES), lambda b, h, i: (b, 0, h)),
            pl.BlockSpec((1, nc, DIFF_V_DIM, tk), lambda b, h, i: (b, 0, h, 0)),
            pl.BlockSpec(lam_vecs.shape, lambda b, h, i: (0, 0)),
            pl.BlockSpec(lam_consts.shape, lambda b, h, i: (0, 0)),
            pl.BlockSpec(subln_col.shape, lambda b, h, i: (0, 0)),
        ],
        out_specs=pl.BlockSpec((1, tq, LANES), lambda b, h, i: (b, i, h)),
        out_shape=jax.ShapeDtypeStruct((B, S, DIFF_HEADS * DIFF_V_DIM), F32),
        scratch_shapes=_attn_scratch(tq, tk, DIFF_V_DIM),
        compiler_params=pltpu.CompilerParams(
            dimension_semantics=("parallel", "parallel", "arbitrary"), vmem_limit_bytes=VMEM_LIMIT),
        name="diff_attn",
    )(qb, kb, vbt, lam_vecs, lam_consts, subln_col)


def _gqa_attn_call(qc, kc, vct):
    B, S, _ = qc.shape
    tq, tk = _attn_tq(S), TM_PROJ
    nc = S // tk
    kern = functools.partial(_gqa_attn_kernel, tq=tq, tk=tk, n_chunks=nc)
    return pl.pallas_call(
        kern,
        grid=(B, GQA_KV_HEADS, S // tq),
        in_specs=[
            pl.BlockSpec((1, tq, LANES), lambda b, g, i: (b, i, g)),
            pl.BlockSpec((1, S, LANES), lambda b, g, i: (b, 0, g)),
            pl.BlockSpec((1, nc, HEAD_DIM, tk), lambda b, g, i: (b, 0, g, 0)),
        ],
        out_specs=pl.BlockSpec((1, tq, LANES), lambda b, g, i: (b, i, g)),
        out_shape=jax.ShapeDtypeStruct((B, S, 2 * LANES), F32),
        scratch_shapes=_attn_scratch(tq, tk, HEAD_DIM),
        compiler_params=pltpu.CompilerParams(
            dimension_semantics=("parallel", "parallel", "arbitrary"), vmem_limit_bytes=VMEM_LIMIT),
        name="gqa_attn",
    )(qc, kc, vct)


def _mix_ffn_kernel(x_ref, xp_ref, xn_ref, a_ref, ap_ref, an_ref,
                    ob_ref, obp_ref, obn_ref, oc_ref, ocp_ref, ocn_ref,
                    caw_ref, cab_ref, lng_ref, lnb_ref, wout_ref, g2_ref,
                    wup_ref, cfw_ref, cfb_ref, wdn_ref,
                    o_ref, as_ref, h2_ref, acc_ref, ua_ref, ub_ref, sha_ref, shb_ref, *, tm, seq_len):
    rows = tm + 2 * HALO
    sh_refs = (sha_ref, shb_ref)
    t0 = pl.program_id(1) * tm

    as_ref[0:8, :] = jnp.zeros((8, CONV_WIDTH), F32)
    as_ref[8:8 + A_HALO, :] = ap_ref[0]
    as_ref[8 + A_HALO:8 + A_HALO + tm, :] = a_ref[0]
    as_ref[8 + A_HALO + tm:8 + 2 * A_HALO + tm, :] = an_ref[0]
    as_ref[8 + 2 * A_HALO + tm:16 + 2 * A_HALO + tm, :] = jnp.zeros((8, CONV_WIDTH), F32)
    arow = lax.broadcasted_iota(jnp.int32, (tm + 2 * A_HALO + 16, 1), 0) + (t0 - A_HALO - 8)
    a_valid = jnp.logical_and(arow >= 0, arow < seq_len)
    as_ref[...] = jnp.where(a_valid, as_ref[...], 0.0)

    conv = jnp.zeros((rows, CONV_WIDTH), F32) + cab_ref[...]
    for r in range(SUBLANES):
        sh_ref = sh_refs[r % 2]
        sh_ref[...] = as_ref[r:r + rows + 3 * SUBLANES, :]
        for q in range(4):
            k = SUBLANES * q + r - 1
            if 0 <= k < CONV_K:
                conv = conv + sh_ref[SUBLANES * q:SUBLANES * q + rows, :] * caw_ref[k:k + 1, :]
    mu = jnp.mean(conv, axis=-1, keepdims=True)
    xc = conv - mu
    var = jnp.mean(xc * xc, axis=-1, keepdims=True)
    ln = xc * lax.rsqrt(var + EPS) * lng_ref[...] + lnb_ref[...]
    a_mix = (ln * _sigmoid(ln)).astype(BF16)

    cat = lambda p, c, n: jnp.concatenate([p[0], c[0], n[0]], axis=0)
    x_ext = cat(xp_ref, x_ref, xn_ref)
    ob = cat(obp_ref, ob_ref, obn_ref).astype(BF16)
    oc = cat(ocp_ref, oc_ref, ocn_ref).astype(BF16)
    x_new = (x_ext
             + jnp.dot(a_mix, wout_ref[0:256, :], preferred_element_type=F32)
             + jnp.dot(ob, wout_ref[256:768, :], preferred_element_type=F32)
             + jnp.dot(oc, wout_ref[768:1024, :], preferred_element_type=F32))

    trow = lax.broadcasted_iota(jnp.int32, (rows, 1), 0) + (t0 - HALO)
    t_valid = jnp.logical_and(trow >= 0, trow < seq_len)
    h2_ref[...] = jnp.where(t_valid, _rms_rows(x_new, g2_ref[...]), 0.0).astype(BF16)
    acc_ref[...] = x_new[HALO:HALO + tm, :]

    def up_proj(j, half):
        return jnp.dot(h2_ref[...], wup_ref[half, j], preferred_element_type=F32)

    def conv3(u, w, b):
        prev = pltpu.roll(u, 1, 0)[HALO:HALO + tm]
        nxt = pltpu.roll(u, rows - 1, 0)[HALO:HALO + tm]
        return prev * w[0:1] + u[HALO:HALO + tm] * w[1:2] + nxt * w[2:3] + b

    def up_to(j, u_ref):
        u_ref[0] = up_proj(j, 0)
        u_ref[1] = up_proj(j, 1)

    def gate_down(j, u_ref):
        cv = conv3(u_ref[0], cfw_ref[0, j], cfb_ref[0, j])
        cg = conv3(u_ref[1], cfw_ref[1, j], cfb_ref[1, j])
        act = (cv * (cg * _sigmoid(cg))).astype(BF16)
        acc_ref[...] += jnp.dot(act, wdn_ref[j], preferred_element_type=F32)

    up_to(0, ua_ref)

    def body(i, carry):
        j = 2 * i
        up_to(j + 1, ub_ref)
        gate_down(j, ua_ref)
        up_to(j + 2, ua_ref)
        gate_down(j + 1, ub_ref)
        return carry

    lax.fori_loop(0, (N_FFN_CHUNKS - 1) // 2, body, 0)
    gate_down(N_FFN_CHUNKS - 1, ua_ref)
    o_ref[0] = acc_ref[...]


def _mix_ffn_call(x, a_glu, ob, oc, caw, cab, lng, lnb, w_out, g2, w_up, cfw, cfb, w_dn):
    B, S, _ = x.shape
    tm = TM_FFN
    nt = S // tm

    def halo_specs(width, halo):
        per = tm // halo
        last = S // halo - 1
        main = pl.BlockSpec((1, tm, width), lambda b, i: (b, i, 0))
        prev = pl.BlockSpec((1, halo, width), lambda b, i: (b, jnp.maximum(i * per - 1, 0), 0))
        nxt = pl.BlockSpec((1, halo, width), lambda b, i: (b, jnp.minimum((i + 1) * per, last), 0))
        return [main, prev, nxt]

    def const(a):
        return pl.BlockSpec(a.shape, lambda b, i: (0,) * a.ndim, pipeline_mode=pl.Buffered(1))

    consts = (caw, cab, lng, lnb, w_out, g2, w_up, cfw, cfb, w_dn)
    kern = functools.partial(_mix_ffn_kernel, tm=tm, seq_len=S)
    return pl.pallas_call(
        kern,
        grid=(B, nt),
        in_specs=(halo_specs(D_MODEL, HALO) + halo_specs(CONV_WIDTH, A_HALO)
                  + halo_specs(512, HALO) + halo_specs(256, HALO)
                  + [const(a) for a in consts]),
        out_specs=pl.BlockSpec((1, tm, D_MODEL), lambda b, i: (b, i, 0)),
        out_shape=jax.ShapeDtypeStruct((B, S, D_MODEL), F32),
        scratch_shapes=[pltpu.VMEM((tm + 2 * A_HALO + 16, CONV_WIDTH), F32),
                        pltpu.VMEM((tm + 2 * HALO, D_MODEL), BF16),
                        pltpu.VMEM((tm, D_MODEL), F32),
                        pltpu.VMEM((2, tm + 2 * HALO, FFN_CHUNK), F32),
                        pltpu.VMEM((2, tm + 2 * HALO, FFN_CHUNK), F32),
                        pltpu.VMEM((tm + 2 * HALO + 3 * SUBLANES, CONV_WIDTH), F32),
                        pltpu.VMEM((tm + 2 * HALO + 3 * SUBLANES, CONV_WIDTH), F32)],
        compiler_params=pltpu.CompilerParams(
            dimension_semantics=("parallel", "parallel"), vmem_limit_bytes=VMEM_LIMIT),
        name="mix_ffn",
    )(x, x, x, a_glu, a_glu, a_glu, ob, ob, ob, oc, oc, oc, *consts)


def _period64(parts):
    blk = jnp.concatenate(parts, axis=-1)
    return jnp.concatenate([blk, blk], axis=-1)


def _rope_tables(S):
    pos = jnp.arange(S, dtype=jnp.int32)

    def angles(p, dim, theta):
        inv = theta ** (-jnp.arange(0, dim, 2, dtype=F32) / dim)
        ang = p.astype(F32)[:, None] * inv[None, :]
        return jnp.cos(ang), jnp.sin(ang)

    cp, sp = angles(pos, PARTIAL_ROT_DIM, ROPE_THETA)
    rest = HEAD_DIM - PARTIAL_ROT_DIM
    zr = jnp.zeros((S, rest), F32)
    z8 = jnp.zeros_like(sp)
    tabs_b = (_period64([cp, cp, jnp.ones((S, rest), F32)]),
              _period64([z8, sp, zr]),
              _period64([-sp, z8, zr]))
    cr, sr = angles(pos // GRID_W, AXIAL_HALF, AXIAL_THETA)
    cc, sc = angles(pos % GRID_W, AXIAL_HALF, AXIAL_THETA)
    z16 = jnp.zeros_like(sr)
    tabs_c = (_period64([cr, cr, cc, cc]),
              _period64([z16, sr, z16, sc]),
              _period64([-sr, z16, -sc, z16]))
    return tabs_b + tabs_c


def _layer_params(l, norm1_g, w_in, conv_a_w, conv_a_b, ln_a_g, ln_a_b, qn_b_g, kn_b_g,
                  lam_q1, lam_k1, lam_q2, lam_k2, subln_b_g, qn_c_g, kn_c_g, w_out,
                  norm2_g, w_up, conv_f_w, conv_f_b, w_down):
    w = w_in[l]
    ck = w[:, 2304:2432]
    w_cols = jnp.concatenate(
        [w[:, 0:1536], w[:, 2048:2304], ck[:, 0:64], ck[:, 0:64], ck[:, 64:128], ck[:, 64:128]],
        axis=1).astype(BF16)
    w_vt = jnp.concatenate([w[:, 1536:2048], w[:, 2432:2560]], axis=1).T.astype(BF16)
    scale = HEAD_DIM ** -0.5 * math.log2(math.e)
    gb = jnp.concatenate([jnp.tile(qn_b_g[l] * scale, 8), jnp.tile(kn_b_g[l], 8)])[None, :]
    gc = jnp.concatenate([jnp.tile(qn_c_g[l] * scale, 4), jnp.tile(kn_c_g[l], 4)])[None, :]
    lam_init = 0.8 - 0.6 * math.exp(-0.3 * l)
    lam_vecs = jnp.stack([lam_q1[l], lam_k1[l], lam_q2[l], lam_k2[l]]).astype(F32)
    lam_consts = jnp.full((SUBLANES, LANES), lam_init, F32)
    wu = w_up[l].astype(BF16).reshape(D_MODEL, 2, N_FFN_CHUNKS, FFN_CHUNK).transpose(1, 2, 0, 3)
    cfw = conv_f_w[l].reshape(3, 2, N_FFN_CHUNKS, FFN_CHUNK).transpose(1, 2, 0, 3)
    cfb = conv_f_b[l].reshape(2, N_FFN_CHUNKS, 1, FFN_CHUNK)
    wd = w_down[l].astype(BF16).reshape(N_FFN_CHUNKS, FFN_CHUNK, D_MODEL)
    return dict(
        g1=norm1_g[l][None, :], w_cols=w_cols, w_vt=w_vt, gb=gb, gc=gc,
        lam_vecs=lam_vecs, lam_consts=lam_consts, subln_col=subln_b_g[l][:, None],
        caw=conv_a_w[l], cab=conv_a_b[l][None, :], lng=ln_a_g[l][None, :], lnb=ln_a_b[l][None, :],
        w_out=w_out[l].astype(BF16), g2=norm2_g[l][None, :], w_up=wu, cfw=cfw, cfb=cfb, w_dn=wd)


def _encoder_layer(x, p, ones, tabs):
    a_glu, qb, kb, vbt, qc, kc, vct = _inproj_call(
        x, p["g1"], p["w_cols"], p["w_vt"], p["gb"], p["gc"], ones, tabs)
    ob = _diff_attn_call(qb, kb, vbt, p["lam_vecs"], p["lam_consts"], p["subln_col"])
    oc = _gqa_attn_call(qc, kc, vct)
    return _mix_ffn_call(x, a_glu, ob, oc, p["caw"], p["cab"], p["lng"], p["lnb"], p["w_out"],
                         p["g2"], p["w_up"], p["cfw"], p["cfb"], p["w_dn"])


def kernel(x_prompt, x_sample, norm1_g, w_in, conv_a_w, conv_a_b, ln_a_g, ln_a_b, qn_b_g, kn_b_g,
           lam_q1, lam_k1, lam_q2, lam_k2, subln_b_g, qn_c_g, kn_c_g, w_out, norm2_g, w_up,
           conv_f_w, conv_f_b, w_down):
    weights = (norm1_g, w_in, conv_a_w, conv_a_b, ln_a_g, ln_a_b, qn_b_g, kn_b_g, lam_q1, lam_k1,
               lam_q2, lam_k2, subln_b_g, qn_c_g, kn_c_g, w_out, norm2_g, w_up, conv_f_w,
               conv_f_b, w_down)
    idx = jnp.arange(256)
    ones = (idx[:, None] // HEAD_DIM == idx[None, :] // HEAD_DIM).astype(BF16)
    tabs_p = _rope_tables(x_prompt.shape[1])
    tabs_s = _rope_tables(x_sample.shape[1])
    y_p, y_s = x_prompt, x_sample
    for l in range(DEPTH):
        p = _layer_params(l, *weights)
        y_p = _encoder_layer(y_p, p, ones, tabs_p)
        y_s = _encoder_layer(y_s, p, ones, tabs_s)
    return (y_p, y_s)
```

```python
import functools
import math

import jax
import jax.numpy as jnp
from jax import lax
from jax.experimental import pallas as pl
from jax.experimental.pallas import tpu as pltpu

F32 = jnp.float32
BF16 = jnp.bfloat16

D_MODEL = 1024
DEPTH = 4
CONV_WIDTH = 256
CONV_K = 31
DIFF_HEADS = 4
HEAD_DIM = 64
DIFF_V_DIM = 128
GQA_KV_HEADS = 2
FFN_HIDDEN = 2816
ROPE_THETA = 500000.0
PARTIAL_ROT_DIM = 16
AXIAL_THETA = 10000.0
AXIAL_HALF = 32
GRID_W = 64
EPS = 1e-6

LANES = 128
SUBLANES = 8
VMEM_LIMIT = 56 * 1024 * 1024

TM_PROJ = 512
TM_FFN = 512
FFN_CHUNK = 256
N_FFN_CHUNKS = FFN_HIDDEN // FFN_CHUNK
HALO = 8
A_HALO = 16

ATTN_UNROLL = 8
TK_SUB = 256
SCORE_LOOKAHEAD = 2
SHIFT_HEADROOM = 80.0
NORM_SLACK = 1.05

NT_DIMS = (((1,), (1,)), ((), ()))


def _rms_rows(x, gain):
    ms = jnp.mean(x * x, axis=-1, keepdims=True)
    return x * lax.rsqrt(ms + EPS) * gain


def _sigmoid(x):
    return 1.0 / (1.0 + jnp.exp(-x))


def _chunk_norm_rope(blk, ones, gain, cos, sin_up, sin_dn, shift):
    sq = blk * blk
    hi = sq.astype(BF16)
    lo = (sq - hi.astype(F32)).astype(BF16)
    ss = (jnp.dot(hi, ones, preferred_element_type=F32)
          + jnp.dot(lo, ones, preferred_element_type=F32))
    y = blk * lax.rsqrt(ss * (1.0 / HEAD_DIM) + EPS) * gain
    pieces = []
    for j in range(blk.shape[1] // LANES):
        yj = y[:, j * LANES:(j + 1) * LANES]
        pieces.append(yj * cos
                      + pltpu.roll(yj, shift, 1) * sin_up
                      + pltpu.roll(yj, LANES - shift, 1) * sin_dn)
    return pieces


def _inproj_kernel(x_ref, g1_ref, w_ref, wvt_ref, gb_ref, gc_ref, ones_ref,
                   cb_ref, sub_ref, sdb_ref, cc_ref, suc_ref, sdc_ref,
                   a_ref, qb_ref, kb_ref, vbt_ref, qc_ref, kc_ref, vct_ref):
    h = _rms_rows(x_ref[0], g1_ref[...]).astype(BF16)

    pa = jnp.dot(h, w_ref[:, 0:512], preferred_element_type=F32)
    a_ref[0] = pa[:, :CONV_WIDTH] * _sigmoid(pa[:, CONV_WIDTH:])

    ones = ones_ref[...]
    cb, sub, sdb = cb_ref[...], sub_ref[...], sdb_ref[...]
    cc, suc, sdc = cc_ref[...], suc_ref[...], sdc_ref[...]

    def project(blk_i):
        c0 = 512 + 256 * blk_i
        return jnp.dot(h, w_ref[:, c0:c0 + 256], preferred_element_type=F32)

    def finish(blk_i, blk):
        if blk_i < 4:
            pieces = _chunk_norm_rope(blk, ones, gb_ref[:, 256 * blk_i:256 * (blk_i + 1)],
                                      cb, sub, sdb, PARTIAL_ROT_DIM // 2)
            dst, lane0 = (qb_ref if blk_i < 2 else kb_ref), (256 * blk_i) % 512
        else:
            pieces = _chunk_norm_rope(blk, ones, gc_ref[:, 256 * (blk_i - 4):256 * (blk_i - 3)],
                                      cc, suc, sdc, AXIAL_HALF // 2)
            dst, lane0 = (qc_ref if blk_i == 4 else kc_ref), 0
        for j, pc in enumerate(pieces):
            dst[0, :, lane0 + LANES * j:lane0 + LANES * (j + 1)] = pc.astype(BF16)

    nxt = project(0)
    for blk_i in range(6):
        blk = nxt
        if blk_i + 1 < 6:
            nxt = project(blk_i + 1)
        else:
            vt = lax.dot_general(wvt_ref[...], h, NT_DIMS, preferred_element_type=F32)
        finish(blk_i, blk)
    vbt_ref[0, 0] = vt[0:512].astype(BF16)
    vct_ref[0, 0] = vt[512:640].astype(BF16)


def _inproj_call(x, g1, w_cols, w_vt, gb, gc, ones, tabs):
    B, S, _ = x.shape
    tm = TM_PROJ
    nt = S // tm
    tok = lambda w: pl.BlockSpec((1, tm, w), lambda b, i: (b, i, 0))
    full = lambda a: pl.BlockSpec(a.shape, lambda b, i: (0,) * a.ndim)
    tab = pl.BlockSpec((tm, LANES), lambda b, i: (i, 0))
    out_shape = (
        jax.ShapeDtypeStruct((B, S, CONV_WIDTH), F32),
        jax.ShapeDtypeStruct((B, S, 512), BF16),
        jax.ShapeDtypeStruct((B, S, 512), BF16),
        jax.ShapeDtypeStruct((B, nt, 512, tm), BF16),
        jax.ShapeDtypeStruct((B, S, 256), BF16),
        jax.ShapeDtypeStruct((B, S, 256), BF16),
        jax.ShapeDtypeStruct((B, nt, 128, tm), BF16),
    )
    out_specs = (
        tok(CONV_WIDTH), tok(512), tok(512),
        pl.BlockSpec((1, 1, 512, tm), lambda b, i: (b, i, 0, 0)),
        tok(256), tok(256),
        pl.BlockSpec((1, 1, 128, tm), lambda b, i: (b, i, 0, 0)),
    )
    return pl.pallas_call(
        _inproj_kernel,
        grid=(B, nt),
        in_specs=[tok(D_MODEL), full(g1), full(w_cols), full(w_vt), full(gb), full(gc), full(ones)]
                 + [tab] * 6,
        out_specs=out_specs,
        out_shape=out_shape,
        compiler_params=pltpu.CompilerParams(
            dimension_semantics=("parallel", "parallel"), vmem_limit_bytes=VMEM_LIMIT),
        name="inproj",
    )(x, g1, w_cols, w_vt, gb, gc, ones, *tabs)


def _flash_core(q_ref, k_ref, vt_ref, qop_ref, kmax_ref, m_ref, l_ref, acc_ref, tq, tk, n_chunks):
    r_id = lax.broadcasted_iota(jnp.int32, (LANES, LANES), 0) // HEAD_DIM
    c_id = lax.broadcasted_iota(jnp.int32, (LANES, LANES), 1) // HEAD_DIM
    half_ones = (r_id == c_id).astype(F32)

    @pl.when(pl.program_id(2) == 0)
    def _key_norm_bound():
        def kb(c, mx):
            k = k_ref[0, pl.ds(pl.multiple_of(c * tk, tk), tk), :].astype(F32)
            n2 = jnp.dot(k * k, half_ones, preferred_element_type=F32)
            return jnp.maximum(mx, jnp.max(n2, axis=0, keepdims=True))
        kmax_ref[...] = lax.fori_loop(0, n_chunks, kb, jnp.zeros((1, LANES), F32))

    q = q_ref[0].astype(F32)
    lane = lax.broadcasted_iota(jnp.int32, q.shape, 1)
    qop_ref[0:tq, :] = jnp.where(lane < HEAD_DIM, q, 0.0).astype(BF16)
    qop_ref[tq:2 * tq, :] = jnp.where(lane >= HEAD_DIM, q, 0.0).astype(BF16)

    def chunk(c):
        k = k_ref[0, pl.ds(pl.multiple_of(c * tk, tk), tk), :]
        return lax.dot_general(k, qop_ref[...], NT_DIMS, preferred_element_type=F32)

    def weigh(c, p):
        return jnp.dot(vt_ref[0, c], p.astype(BF16), preferred_element_type=F32)

    s0 = lax.dot_general(k_ref[0, 0:TK_SUB, :], qop_ref[...], NT_DIMS, preferred_element_type=F32)
    m0 = jnp.max(s0, axis=0, keepdims=True)
    m_ref[...] = m0
    l_ref[...] = jnp.zeros(l_ref.shape, F32)
    acc_ref[...] = jnp.zeros(acc_ref.shape, F32)

    q2 = jnp.max(jnp.max(jnp.dot(q * q, half_ones, preferred_element_type=F32),
                         axis=0, keepdims=True), axis=1, keepdims=True)
    k2 = jnp.max(kmax_ref[...], axis=1, keepdims=True)
    room = SHIFT_HEADROOM + jnp.min(m0, axis=1, keepdims=True)
    safe = jnp.logical_and(room > 0.0, q2 * k2 * NORM_SLACK <= room * room)
    n_safe = jnp.sum(safe.astype(jnp.int32))

    n_sub = tk // TK_SUB

    def frozen_block(c0, n_blk):
        m = m_ref[...]
        subs = [(ci, j) for ci in range(n_blk) for j in range(n_sub)]

        def sub_scores(ci, j):
            row0 = pl.multiple_of((c0 + ci) * tk + j * TK_SUB, TK_SUB)
            k = k_ref[0, pl.ds(row0, TK_SUB), :]
            return lax.dot_general(k, qop_ref[...], NT_DIMS, preferred_element_type=F32)

        pending = {t: sub_scores(*subs[t]) for t in range(min(SCORE_LOOKAHEAD, len(subs)))}
        l_add = jnp.zeros(m.shape, F32)
        ps = []
        for t, (ci, j) in enumerate(subs):
            if t + SCORE_LOOKAHEAD < len(subs):
                pending[t + SCORE_LOOKAHEAD] = sub_scores(*subs[t + SCORE_LOOKAHEAD])
            p = jnp.exp2(pending.pop(t) - m)
            l_add = l_add + jnp.sum(p, axis=0, keepdims=True)
            ps.append(p.astype(BF16))
            if j == n_sub - 1:
                acc_ref[...] += weigh(c0 + ci, jnp.concatenate(ps, axis=0))
                ps = []
        l_ref[...] += l_add

    def online_body(c, carry):
        s = chunk(c)
        m_old = m_ref[...]
        m_new = jnp.maximum(m_old, jnp.max(s, axis=0, keepdims=True))
        alpha = jnp.exp2(m_old - m_new)
        p = jnp.exp2(s - m_new)
        l_ref[...] = alpha * l_ref[...] + jnp.sum(p, axis=0, keepdims=True)
        acc_ref[...] = alpha * acc_ref[...] + weigh(c, p)
        m_ref[...] = m_new
        return carry

    @pl.when(n_safe > 0)
    def _frozen_shift():
        n_full, n_tail = divmod(n_chunks, ATTN_UNROLL)

        def full_block(i, carry):
            frozen_block(i * ATTN_UNROLL, ATTN_UNROLL)
            return carry

        if n_full > 1:
            lax.fori_loop(0, n_full, full_block, 0)
        elif n_full == 1:
            frozen_block(0, ATTN_UNROLL)
        if n_tail > 0:
            frozen_block(n_full * ATTN_UNROLL, n_tail)

    @pl.when(n_safe == 0)
    def _online_shift():
        lax.fori_loop(0, n_chunks, online_body, 0)

    return acc_ref[...] * (1.0 / l_ref[...])


def _run_flash(q_ref, k_ref, vt_ref, scratch, tq, tk, n_chunks):
    qop_ref, kmax_ref, m_ref, l_ref, acc_ref = scratch
    return _flash_core(q_ref, k_ref, vt_ref, qop_ref, kmax_ref, m_ref, l_ref, acc_ref, tq, tk, n_chunks)


def _diff_attn_kernel(q_ref, k_ref, vt_ref, lam_ref, lamc_ref, g_ref, o_ref, *scratch, tq, tk, n_chunks):
    o = _run_flash(q_ref, k_ref, vt_ref, scratch, tq, tk, n_chunks)
    lam_init = lamc_ref[0:1, 0:1]
    lam = (jnp.exp(jnp.sum(lam_ref[0:1, :] * lam_ref[1:2, :], axis=-1, keepdims=True))
           - jnp.exp(jnp.sum(lam_ref[2:3, :] * lam_ref[3:4, :], axis=-1, keepdims=True))
           + lam_init)
    od = o[:, 0:tq] - lam * o[:, tq:2 * tq]
    ms = jnp.mean(od * od, axis=0, keepdims=True)
    y = od * lax.rsqrt(ms + EPS) * g_ref[...] * (1.0 - lam_init)
    o_ref[0] = y.T


def _gqa_attn_kernel(q_ref, k_ref, vt_ref, o_ref, *scratch, tq, tk, n_chunks):
    o = _run_flash(q_ref, k_ref, vt_ref, scratch, tq, tk, n_chunks)
    o_ref[0] = jnp.concatenate([o[:, 0:tq], o[:, tq:2 * tq]], axis=0).T


def _attn_tq(S):
    return 512


def _attn_scratch(tq, tk, dv):
    row = pltpu.VMEM((1, 2 * tq), F32)
    return [pltpu.VMEM((2 * tq, LANES), BF16), pltpu.VMEM((1, LANES), F32), row, row,
            pltpu.VMEM((dv, 2 * tq), F32)]


def _diff_attn_call(qb, kb, vbt, lam_vecs, lam_consts, subln_col):
    B, S, _ = qb.shape
    tq, tk = _attn_tq(S), TM_PROJ
    nc = S // tk
    kern = functools.partial(_diff_attn_kernel, tq=tq, tk=tk, n_chunks=nc)
    return pl.pallas_call(
        kern,
        grid=(B, DIFF_HEADS, S // tq),
        in_specs=[
            pl.BlockSpec((1, tq, LANES), lambda b, h, i: (b, i, h)),
            pl.BlockSpec((1, S, LANES), lambda b, h, i: (b, 0, h)),
            pl.BlockSpec((1, nc, DIFF_V_DIM, tk), lambda b, h, i: (b, 0, h, 0)),
            pl.BlockSpec(lam_vecs.shape, lambda b, h, i: (0, 0)),
            pl.BlockSpec(lam_consts.shape, lambda b, h, i: (0, 0)),
            pl.BlockSpec(subln_col.shape, lambda b, h, i: (0, 0)),
        ],
        out_specs=pl.BlockSpec((1, tq, LANES), lambda b, h, i: (b, i, h)),
        out_shape=jax.ShapeDtypeStruct((B, S, DIFF_HEADS * DIFF_V_DIM), F32),
        scratch_shapes=_attn_scratch(tq, tk, DIFF_V_DIM),
        compiler_params=pltpu.CompilerParams(
            dimension_semantics=("parallel", "parallel", "arbitrary"), vmem_limit_bytes=VMEM_LIMIT),
        name="diff_attn",
    )(qb, kb, vbt, lam_vecs, lam_consts, subln_col)


def _gqa_attn_call(qc, kc, vct):
    B, S, _ = qc.shape
    tq, tk = _attn_tq(S), TM_PROJ
    nc = S // tk
    kern = functools.partial(_gqa_attn_kernel, tq=tq, tk=tk, n_chunks=nc)
    return pl.pallas_call(
        kern,
        grid=(B, GQA_KV_HEADS, S // tq),
        in_specs=[
            pl.BlockSpec((1, tq, LANES), lambda b, g, i: (b, i, g)),
            pl.BlockSpec((1, S, LANES), lambda b, g, i: (b, 0, g)),
            pl.BlockSpec((1, nc, HEAD_DIM, tk), lambda b, g, i: (b, 0, g, 0)),
        ],
        out_specs=pl.BlockSpec((1, tq, LANES), lambda b, g, i: (b, i, g)),
        out_shape=jax.ShapeDtypeStruct((B, S, 2 * LANES), F32),
        scratch_shapes=_attn_scratch(tq, tk, HEAD_DIM),
        compiler_params=pltpu.CompilerParams(
            dimension_semantics=("parallel", "parallel", "arbitrary"), vmem_limit_bytes=VMEM_LIMIT),
        name="gqa_attn",
    )(qc, kc, vct)


def _mix_ffn_kernel(x_ref, xp_ref, xn_ref, a_ref, ap_ref, an_ref,
                    ob_ref, obp_ref, obn_ref, oc_ref, ocp_ref, ocn_ref,
                    caw_ref, cab_ref, lng_ref, lnb_ref, wout_ref, g2_ref,
                    wup_ref, cfw_ref, cfb_ref, wdn_ref,
                    o_ref, as_ref, h2_ref, acc_ref, ua_ref, ub_ref, sha_ref, shb_ref, *, tm, seq_len):
    rows = tm + 2 * HALO
    sh_refs = (sha_ref, shb_ref)
    t0 = pl.program_id(1) * tm

    as_ref[0:8, :] = jnp.zeros((8, CONV_WIDTH), F32)
    as_ref[8:8 + A_HALO, :] = ap_ref[0]
    as_ref[8 + A_HALO:8 + A_HALO + tm, :] = a_ref[0]
    as_ref[8 + A_HALO + tm:8 + 2 * A_HALO + tm, :] = an_ref[0]
    as_ref[8 + 2 * A_HALO + tm:16 + 2 * A_HALO + tm, :] = jnp.zeros((8, CONV_WIDTH), F32)
    arow = lax.broadcasted_iota(jnp.int32, (tm + 2 * A_HALO + 16, 1), 0) + (t0 - A_HALO - 8)
    a_valid = jnp.logical_and(arow >= 0, arow < seq_len)
    as_ref[...] = jnp.where(a_valid, as_ref[...], 0.0)

    conv = jnp.zeros((rows, CONV_WIDTH), F32) + cab_ref[...]
    for r in range(SUBLANES):
        sh_ref = sh_refs[r % 2]
        sh_ref[...] = as_ref[r:r + rows + 3 * SUBLANES, :]
        for q in range(4):
            k = SUBLANES * q + r - 1
            if 0 <= k < CONV_K:
                conv = conv + sh_ref[SUBLANES * q:SUBLANES * q + rows, :] * caw_ref[k:k + 1, :]
    mu = jnp.mean(conv, axis=-1, keepdims=True)
    xc = conv - mu
    var = jnp.mean(xc * xc, axis=-1, keepdims=True)
    ln = xc * lax.rsqrt(var + EPS) * lng_ref[...] + lnb_ref[...]
    a_mix = (ln * _sigmoid(ln)).astype(BF16)

    cat = lambda p, c, n: jnp.concatenate([p[0], c[0], n[0]], axis=0)
    x_ext = cat(xp_ref, x_ref, xn_ref)
    ob = cat(obp_ref, ob_ref, obn_ref).astype(BF16)
    oc = cat(ocp_ref, oc_ref, ocn_ref).astype(BF16)
    x_new = (x_ext
             + jnp.dot(a_mix, wout_ref[0:256, :], preferred_element_type=F32)
             + jnp.dot(ob, wout_ref[256:768, :], preferred_element_type=F32)
             + jnp.dot(oc, wout_ref[768:1024, :], preferred_element_type=F32))

    trow = lax.broadcasted_iota(jnp.int32, (rows, 1), 0) + (t0 - HALO)
    t_valid = jnp.logical_and(trow >= 0, trow < seq_len)
    h2_ref[...] = jnp.where(t_valid, _rms_rows(x_new, g2_ref[...]), 0.0).astype(BF16)
    acc_ref[...] = x_new[HALO:HALO + tm, :]

    def up_proj(j, half):
        return jnp.dot(h2_ref[...], wup_ref[half, j], preferred_element_type=F32)

    def conv3(u, w, b):
        prev = pltpu.roll(u, 1, 0)[HALO:HALO + tm]
        nxt = pltpu.roll(u, rows - 1, 0)[HALO:HALO + tm]
        return prev * w[0:1] + u[HALO:HALO + tm] * w[1:2] + nxt * w[2:3] + b

    def up_to(j, u_ref):
        u_ref[0] = up_proj(j, 0)
        u_ref[1] = up_proj(j, 1)

    def gate_down(j, u_ref):
        cv = conv3(u_ref[0], cfw_ref[0, j], cfb_ref[0, j])
        cg = conv3(u_ref[1], cfw_ref[1, j], cfb_ref[1, j])
        act = (cv * (cg * _sigmoid(cg))).astype(BF16)
        acc_ref[...] += jnp.dot(act, wdn_ref[j], preferred_element_type=F32)

    up_to(0, ua_ref)

    def body(i, carry):
        j = 2 * i
        up_to(j + 1, ub_ref)
        gate_down(j, ua_ref)
        up_to(j + 2, ua_ref)
        gate_down(j + 1, ub_ref)
        return carry

    lax.fori_loop(0, (N_FFN_CHUNKS - 1) // 2, body, 0)
    gate_down(N_FFN_CHUNKS - 1, ua_ref)
    o_ref[0] = acc_ref[...]


def _mix_ffn_call(x, a_glu, ob, oc, caw, cab, lng, lnb, w_out, g2, w_up, cfw, cfb, w_dn):
    B, S, _ = x.shape
    tm = TM_FFN
    nt = S // tm

    def halo_specs(width, halo):
        per = tm // halo
        last = S // halo - 1
        main = pl.BlockSpec((1, tm, width), lambda b, i: (b, i, 0))
        prev = pl.BlockSpec((1, halo, width), lambda b, i: (b, jnp.maximum(i * per - 1, 0), 0))
        nxt = pl.BlockSpec((1, halo, width), lambda b, i: (b, jnp.minimum((i + 1) * per, last), 0))
        return [main, prev, nxt]

    def const(a):
        return pl.BlockSpec(a.shape, lambda b, i: (0,) * a.ndim, pipeline_mode=pl.Buffered(1))

    consts = (caw, cab, lng, lnb, w_out, g2, w_up, cfw, cfb, w_dn)
    kern = functools.partial(_mix_ffn_kernel, tm=tm, seq_len=S)
    return pl.pallas_call(
        kern,
        grid=(B, nt),
        in_specs=(halo_specs(D_MODEL, HALO) + halo_specs(CONV_WIDTH, A_HALO)
                  + halo_specs(512, HALO) + halo_specs(256, HALO)
                  + [const(a) for a in consts]),
        out_specs=pl.BlockSpec((1, tm, D_MODEL), lambda b, i: (b, i, 0)),
        out_shape=jax.ShapeDtypeStruct((B, S, D_MODEL), F32),
        scratch_shapes=[pltpu.VMEM((tm + 2 * A_HALO + 16, CONV_WIDTH), F32),
                        pltpu.VMEM((tm + 2 * HALO, D_MODEL), BF16),
                        pltpu.VMEM((tm, D_MODEL), F32),
                        pltpu.VMEM((2, tm + 2 * HALO, FFN_CHUNK), F32),
                        pltpu.VMEM((2, tm + 2 * HALO, FFN_CHUNK), F32),
                        pltpu.VMEM((tm + 2 * HALO + 3 * SUBLANES, CONV_WIDTH), F32),
                        pltpu.VMEM((tm + 2 * HALO + 3 * SUBLANES, CONV_WIDTH), F32)],
        compiler_params=pltpu.CompilerParams(
            dimension_semantics=("parallel", "parallel"), vmem_limit_bytes=VMEM_LIMIT),
        name="mix_ffn",
    )(x, x, x, a_glu, a_glu, a_glu, ob, ob, ob, oc, oc, oc, *consts)


def _period64(parts):
    blk = jnp.concatenate(parts, axis=-1)
    return jnp.concatenate([blk, blk], axis=-1)


def _rope_tables(S):
    pos = jnp.arange(S, dtype=jnp.int32)

    def angles(p, dim, theta):
        inv = theta ** (-jnp.arange(0, dim, 2, dtype=F32) / dim)
        ang = p.astype(F32)[:, None] * inv[None, :]
        return jnp.cos(ang), jnp.sin(ang)

    cp, sp = angles(pos, PARTIAL_ROT_DIM, ROPE_THETA)
    rest = HEAD_DIM - PARTIAL_ROT_DIM
    zr = jnp.zeros((S, rest), F32)
    z8 = jnp.zeros_like(sp)
    tabs_b = (_period64([cp, cp, jnp.ones((S, rest), F32)]),
              _period64([z8, sp, zr]),
              _period64([-sp, z8, zr]))
    cr, sr = angles(pos // GRID_W, AXIAL_HALF, AXIAL_THETA)
    cc, sc = angles(pos % GRID_W, AXIAL_HALF, AXIAL_THETA)
    z16 = jnp.zeros_like(sr)
    tabs_c = (_period64([cr, cr, cc, cc]),
              _period64([z16, sr, z16, sc]),
              _period64([-sr, z16, -sc, z16]))
    return tabs_b + tabs_c


def _layer_params(l, norm1_g, w_in, conv_a_w, conv_a_b, ln_a_g, ln_a_b, qn_b_g, kn_b_g,
                  lam_q1, lam_k1, lam_q2, lam_k2, subln_b_g, qn_c_g, kn_c_g, w_out,
                  norm2_g, w_up, conv_f_w, conv_f_b, w_down):
    w = w_in[l]
    ck = w[:, 2304:2432]
    w_cols = jnp.concatenate(
        [w[:, 0:1536], w[:, 2048:2304], ck[:, 0:64], ck[:, 0:64], ck[:, 64:128], ck[:, 64:128]],
        axis=1).astype(BF16)
    w_vt = jnp.concatenate([w[:, 1536:2048], w[:, 2432:2560]], axis=1).T.astype(BF16)
    scale = HEAD_DIM ** -0.5 * math.log2(math.e)
    gb = jnp.concatenate([jnp.tile(qn_b_g[l] * scale, 8), jnp.tile(kn_b_g[l], 8)])[None, :]
    gc = jnp.concatenate([jnp.tile(qn_c_g[l] * scale, 4), jnp.tile(kn_c_g[l], 4)])[None, :]
    lam_init = 0.8 - 0.6 * math.exp(-0.3 * l)
    lam_vecs = jnp.stack([lam_q1[l], lam_k1[l], lam_q2[l], lam_k2[l]]).astype(F32)
    lam_consts = jnp.full((SUBLANES, LANES), lam_init, F32)
    wu = w_up[l].astype(BF16).reshape(D_MODEL, 2, N_FFN_CHUNKS, FFN_CHUNK).transpose(1, 2, 0, 3)
    cfw = conv_f_w[l].reshape(3, 2, N_FFN_CHUNKS, FFN_CHUNK).transpose(1, 2, 0, 3)
    cfb = conv_f_b[l].reshape(2, N_FFN_CHUNKS, 1, FFN_CHUNK)
    wd = w_down[l].astype(BF16).reshape(N_FFN_CHUNKS, FFN_CHUNK, D_MODEL)
    return dict(
        g1=norm1_g[l][None, :], w_cols=w_cols, w_vt=w_vt, gb=gb, gc=gc,
        lam_vecs=lam_vecs, lam_consts=lam_consts, subln_col=subln_b_g[l][:, None],
        caw=conv_a_w[l], cab=conv_a_b[l][None, :], lng=ln_a_g[l][None, :], lnb=ln_a_b[l][None, :],
        w_out=w_out[l].astype(BF16), g2=norm2_g[l][None, :], w_up=wu, cfw=cfw, cfb=cfb, w_dn=wd)


def _encoder_layer(x, p, ones, tabs):
    a_glu, qb, kb, vbt, qc, kc, vct = _inproj_call(
        x, p["g1"], p["w_cols"], p["w_vt"], p["gb"], p["gc"], ones, tabs)
    ob = _diff_attn_call(qb, kb, vbt, p["lam_vecs"], p["lam_consts"], p["subln_col"])
    oc = _gqa_attn_call(qc, kc, vct)
    return _mix_ffn_call(x, a_glu, ob, oc, p["caw"], p["cab"], p["lng"], p["lnb"], p["w_out"],
                         p["g2"], p["w_up"], p["cfw"], p["cfb"], p["w_dn"])


def kernel(x_prompt, x_sample, norm1_g, w_in, conv_a_w, conv_a_b, ln_a_g, ln_a_b, qn_b_g, kn_b_g,
           lam_q1, lam_k1, lam_q2, lam_k2, subln_b_g, qn_c_g, kn_c_g, w_out, norm2_g, w_up,
           conv_f_w, conv_f_b, w_down):
    weights = (norm1_g, w_in, conv_a_w, conv_a_b, ln_a_g, ln_a_b, qn_b_g, kn_b_g, lam_q1, lam_k1,
               lam_q2, lam_k2, subln_b_g, qn_c_g, kn_c_g, w_out, norm2_g, w_up, conv_f_w,
               conv_f_b, w_down)
    idx = jnp.arange(256)
    ones = (idx[:, None] // HEAD_DIM == idx[None, :] // HEAD_DIM).astype(BF16)
    tabs_p = _rope_tables(x_prompt.shape[1])
    tabs_s = _rope_tables(x_sample.shape[1])
    y_p, y_s = x_prompt, x_sample
    for l in range(DEPTH):
        p = _layer_params(l, *weights)
        y_p = _encoder_layer(y_p, p, ones, tabs_p)
        y_s = _encoder_layer(y_s, p, ones, tabs_s)
    return (y_p, y_s)
```
